```python
import jax, jax.numpy as jnp
from jax import lax
import numpy as np

D_MODEL = 1024
BATCH = 32
SEQ = 2048
DEPTH = 1
DEC_BATCH = 2
DEC_SEQ = 8192
PAST_LEN = 128

GRID_W = 64
HEAD_DIM = 64
N_HEADS_A = 8
N_KV_HEADS_A = 2
N_HEADS_B = 8
WIDTH_A = N_HEADS_A * HEAD_DIM
WIDTH_KV_A = N_KV_HEADS_A * HEAD_DIM
WIDTH_B = N_HEADS_B * HEAD_DIM
NA_MAX_ROWS = 8
NA_COLS = 16
Q_BLOCK = 128
ROPE_THETA = 10000.0
D_FF = 2816
CONV_WIDTH = 3
EPS = 1e-6
SPLIT_SIZES = [WIDTH_A, WIDTH_KV_A, WIDTH_KV_A, WIDTH_B, WIDTH_B, WIDTH_B, D_MODEL, D_MODEL]
D_IN = sum(SPLIT_SIZES)
SPLIT_POINTS = [int(v) for v in np.cumsum(SPLIT_SIZES)[:-1]]

kernel_name = "hybrid_gqa_natten_gated_encoder"


def rms_norm(x, w):
    xf = x.astype(jnp.float32)
    y = xf * lax.rsqrt(jnp.mean(xf * xf, axis=-1, keepdims=True) + EPS)
    return (y * w.astype(jnp.float32)).astype(x.dtype)


def axial_rope_tables(seq_len, dtype):
    t = np.arange(seq_len)
    row = (t // GRID_W).astype(np.float32)
    col = (t % GRID_W).astype(np.float32)
    half = HEAD_DIM // 2
    freqs = (ROPE_THETA ** (-np.arange(0, half, 2, dtype=np.float32) / half)).astype(np.float32)
    ang_r = row[:, None] * freqs[None, :]
    ang_c = col[:, None] * freqs[None, :]
    ang = np.concatenate([ang_r, ang_r, ang_c, ang_c], axis=-1).astype(np.float32)
    return jnp.asarray(np.cos(ang), dtype), jnp.asarray(np.sin(ang), dtype)


def rotate_half(x):
    h = x.shape[-1] // 2
    return jnp.concatenate([-x[..., h:], x[..., :h]], axis=-1)


def apply_axial_rope(x, cos, sin):
    half = HEAD_DIM // 2
    rot = jnp.concatenate([rotate_half(x[..., :half]), rotate_half(x[..., half:])], axis=-1)
    return x * cos[:, None, :] + rot * sin[:, None, :]


def gqa_branch(q, k, v, q_norm_w, k_norm_w):
    B, S = q.shape[:2]
    G = N_HEADS_A // N_KV_HEADS_A
    q = rms_norm(q.reshape(B, S, N_HEADS_A, HEAD_DIM), q_norm_w)
    k = rms_norm(k.reshape(B, S, N_KV_HEADS_A, HEAD_DIM), k_norm_w)
    v = v.reshape(B, S, N_KV_HEADS_A, HEAD_DIM)
    cos, sin = axial_rope_tables(S, q.dtype)
    q = apply_axial_rope(q, cos, sin)
    k = apply_axial_rope(k, cos, sin)
    n_blocks = S // Q_BLOCK
    qb = q.reshape(B, n_blocks, Q_BLOCK, N_KV_HEADS_A, G, HEAD_DIM).transpose(1, 0, 2, 3, 4, 5)
    scale = HEAD_DIM ** -0.5

    def attend_block(q_blk):
        s = jnp.einsum('bqkgd,bskd->bkgqs', q_blk, k).astype(jnp.float32) * scale
        p = jax.nn.softmax(s, axis=-1).astype(v.dtype)
        return jnp.einsum('bkgqs,bskd->bqkgd', p, v)

    o = lax.map(attend_block, qb)
    return o.transpose(1, 0, 2, 3, 4, 5).reshape(B, S, WIDTH_A)


def neighbourhood_branch(q, k, v, rpb):
    B, S = q.shape[:2]
    rows = S // GRID_W
    wr = min(NA_MAX_ROWS, rows)
    qg = q.reshape(B, rows, GRID_W, N_HEADS_B, HEAD_DIM).transpose(1, 0, 2, 3, 4)
    kg = k.reshape(B, rows, GRID_W, N_HEADS_B, HEAD_DIM)
    vg = v.reshape(B, rows, GRID_W, N_HEADS_B, HEAD_DIM)
    r = np.arange(rows)
    row_start = np.clip(r - wr // 2, 0, rows - wr).astype(np.int32)
    row_off_idx = (row_start[:, None] + np.arange(wr)[None, :] - r[:, None] + NA_MAX_ROWS - 1).astype(np.int32)
    c = np.arange(GRID_W)
    col_start = np.clip(c - NA_COLS // 2, 0, GRID_W - NA_COLS)
    in_win = (c[None, :] >= col_start[:, None]) & (c[None, :] < col_start[:, None] + NA_COLS)
    col_idx = np.clip(c[None, :] - c[:, None], -(NA_COLS - 1), NA_COLS - 1) + NA_COLS - 1
    col_mask = jnp.asarray(in_win)
    rpb_cols = rpb[:, :, col_idx]
    scale = HEAD_DIM ** -0.5

    def row_block(args):
        q_row, start, off_idx = args
        k_band = lax.dynamic_slice_in_dim(kg, start, wr, axis=1)
        v_band = lax.dynamic_slice_in_dim(vg, start, wr, axis=1)
        s = jnp.einsum('bqhd,bwkhd->bhqwk', q_row, k_band).astype(jnp.float32) * scale
        bias = jnp.take(rpb_cols, off_idx, axis=1).transpose(0, 2, 1, 3)
        s = s + bias.astype(jnp.float32)[None]
        s = jnp.where(col_mask[None, None, :, None, :], s, -jnp.inf)
        p = jax.nn.softmax(s, axis=(-2, -1)).astype(v_band.dtype)
        return jnp.einsum('bhqwk,bwkhd->bqhd', p, v_band)

    o = lax.map(row_block, (qg, jnp.asarray(row_start), jnp.asarray(row_off_idx)))
    return o.transpose(1, 0, 2, 3, 4).reshape(B, S, WIDTH_B)


def depthwise_conv_centred(u, w, b):
    pad = CONV_WIDTH // 2
    S = u.shape[1]
    up = jnp.pad(u, ((0, 0), (pad, pad), (0, 0)))
    out = up[:, 0:S] * w[0] + b
    for j in range(1, CONV_WIDTH):
        out = out + up[:, j:j + S] * w[j]
    return out


def encoder_layer(x, pre_mix_norm, w_in, b_gate, q_norm, k_norm, rpb, w_proj_a, w_proj_b, w_out,
                  post_mix_norm, pre_ffn_norm, w_up, conv_w, conv_b, w_down, post_ffn_norm):
    h = rms_norm(x, pre_mix_norm)
    proj = h @ w_in
    qa, ka, va, qb, kb, vb, g_a, g_b = jnp.split(proj, SPLIT_POINTS, axis=-1)
    gate_a = jax.nn.sigmoid((g_a + b_gate[:D_MODEL]).astype(jnp.float32)).astype(x.dtype)
    gate_b = jax.nn.sigmoid((g_b + b_gate[D_MODEL:]).astype(jnp.float32)).astype(x.dtype)
    o_a = gqa_branch(qa, ka, va, q_norm, k_norm) @ w_proj_a
    o_b = neighbourhood_branch(qb, kb, vb, rpb) @ w_proj_b
    mix = (gate_a * o_a + gate_b * o_b) @ w_out
    x = x + rms_norm(mix, post_mix_norm)
    h = rms_norm(x, pre_ffn_norm)
    u = depthwise_conv_centred(h @ w_up, conv_w, conv_b)
    gate, val = u[..., :D_FF], u[..., D_FF:]
    f = (jax.nn.gelu(gate, approximate=True) * val) @ w_down
    return x + rms_norm(f, post_ffn_norm)


def setup_inputs(seed: int = 0) -> dict:
    key = jax.random.key(seed)
    ks = jax.random.split(key, 20)
    f32 = jnp.float32

    def nrm(k, shape, scale):
        return jax.random.normal(k, shape, f32) * scale

    def gain(k, shape):
        return 1.0 + 0.01 * jax.random.normal(k, shape, f32)

    return {
        "x_prompt": nrm(ks[0], (BATCH, SEQ, D_MODEL), 1.0),
        "x_sample": nrm(ks[1], (DEC_BATCH, DEC_SEQ, D_MODEL), 1.0),
        "pre_mix_norm": gain(ks[2], (DEPTH, D_MODEL)),
        "w_in": nrm(ks[3], (DEPTH, D_MODEL, D_IN), D_MODEL ** -0.5),
        "b_gate": nrm(ks[4], (DEPTH, 2 * D_MODEL), 0.01),
        "q_norm": gain(ks[5], (DEPTH, HEAD_DIM)),
        "k_norm": gain(ks[6], (DEPTH, HEAD_DIM)),
        "rpb": nrm(ks[7], (DEPTH, N_HEADS_B, 2 * NA_MAX_ROWS - 1, 2 * NA_COLS - 1), 0.02),
        "w_proj_a": nrm(ks[8], (DEPTH, WIDTH_A, D_MODEL), WIDTH_A ** -0.5),
        "w_proj_b": nrm(ks[9], (DEPTH, WIDTH_B, D_MODEL), WIDTH_B ** -0.5),
        "w_out": nrm(ks[10], (DEPTH, D_MODEL, D_MODEL), D_MODEL ** -0.5),
        "post_mix_norm": gain(ks[11], (DEPTH, D_MODEL)),
        "pre_ffn_norm": gain(ks[12], (DEPTH, D_MODEL)),
        "w_up": nrm(ks[13], (DEPTH, D_MODEL, 2 * D_FF), D_MODEL ** -0.5),
        "conv_w": nrm(ks[14], (DEPTH, CONV_WIDTH, 2 * D_FF), CONV_WIDTH ** -0.5),
        "conv_b": nrm(ks[15], (DEPTH, 2 * D_FF), 0.01),
        "w_down": nrm(ks[16], (DEPTH, D_FF, D_MODEL), D_FF ** -0.5),
        "post_ffn_norm": gain(ks[17], (DEPTH, D_MODEL)),
    }


def reference(x_prompt, x_sample, pre_mix_norm, w_in, b_gate, q_norm, k_norm, rpb, w_proj_a, w_proj_b,
              w_out, post_mix_norm, pre_ffn_norm, w_up, conv_w, conv_b, w_down, post_ffn_norm):
    def run_trunk(x):
        for l in range(DEPTH):
            x = encoder_layer(x, pre_mix_norm[l], w_in[l], b_gate[l], q_norm[l], k_norm[l], rpb[l],
                              w_proj_a[l], w_proj_b[l], w_out[l], post_mix_norm[l], pre_ffn_norm[l],
                              w_up[l], conv_w[l], conv_b[l], w_down[l], post_ffn_norm[l])
        return x

    y_prompt = run_trunk(x_prompt)
    y_sample = run_trunk(x_sample)
    return (y_prompt, y_sample)
```

```python
import functools

import numpy as np
import jax
import jax.numpy as jnp
from jax import lax
from jax.experimental import pallas as pl
from jax.experimental.pallas import tpu as pltpu

D_MODEL = 1024
GRID_W = 64
HEAD_DIM = 64
N_HEADS_A = 8
N_KV_HEADS_A = 2
N_HEADS_B = 8
WIDTH_A = N_HEADS_A * HEAD_DIM
WIDTH_KV_A = N_KV_HEADS_A * HEAD_DIM
WIDTH_B = N_HEADS_B * HEAD_DIM
NA_MAX_ROWS = 8
NA_COLS = 16
ROPE_THETA = 10000.0
D_FF = 2816
CONV_WIDTH = 3
EPS = 1e-6
SCALE = HEAD_DIM ** -0.5
GROUP_A = N_HEADS_A // N_KV_HEADS_A
WIDTH_QKV_A = WIDTH_A + 2 * WIDTH_KV_A
WIDTH_QKV_B = 3 * WIDTH_B

F32 = jnp.float32
BF16 = jnp.bfloat16
MIB = 1024 * 1024

SUBLANES = 8
LANES = 128

PROJ_TILE = 512
GQA_TQ = 256
GQA_TK = 512
NA_BLOCK_ROWS = 32
FFN_CHUNK = 256
HALO = SUBLANES


def _const_spec(shape):
    n = len(shape)
    return pl.BlockSpec(shape, lambda *_: (0,) * n, pipeline_mode=pl.Buffered(1))


def _params(n_axes, vmem_mib):
    return pltpu.CompilerParams(dimension_semantics=("parallel",) * n_axes,
                                vmem_limit_bytes=vmem_mib * MIB)


def _rms(x, w):
    return x * lax.rsqrt(jnp.mean(x * x, axis=-1, keepdims=True) + EPS) * w


def _proj_kernel(x_ref, nw_ref, wat_ref, wb_ref, cos_ref, sin_ref, qn_ref, kn_ref,
                 qt_ref, k_ref, vt_ref, qb_ref, kb_ref, vb_ref):
    h = _rms(x_ref[...], nw_ref[...]).astype(BF16)
    pa = lax.dot_general(wat_ref[...], h, (((1,), (1,)), ((), ())), preferred_element_type=F32)
    cos = cos_ref[...]
    sin = sin_ref[...]

    def norm_rope(t, w):
        t = t * lax.rsqrt(jnp.mean(t * t, axis=0, keepdims=True) + EPS) * w
        q4 = HEAD_DIM // 4
        rot = jnp.concatenate([-t[q4:2 * q4], t[0:q4], -t[3 * q4:4 * q4], t[2 * q4:3 * q4]], axis=0)
        return t * cos + rot * sin

    qn = qn_ref[...]
    kn = kn_ref[...]
    for hh in range(N_HEADS_A):
        lo = hh * HEAD_DIM
        qt_ref[lo:lo + HEAD_DIM, :] = (norm_rope(pa[lo:lo + HEAD_DIM], qn) * SCALE).astype(BF16)
    kt = jnp.concatenate(
        [norm_rope(pa[WIDTH_A + g * HEAD_DIM:WIDTH_A + (g + 1) * HEAD_DIM], kn) for g in range(N_KV_HEADS_A)],
        axis=0)
    k_ref[...] = kt.T.astype(BF16)
    vt_ref[...] = pa[WIDTH_A + WIDTH_KV_A:].astype(BF16)

    pb = jnp.dot(h, wb_ref[...], preferred_element_type=F32)
    qb_ref[...] = (pb[:, :WIDTH_B] * SCALE).astype(BF16)
    kb_ref[...] = pb[:, WIDTH_B:2 * WIDTH_B].astype(BF16)
    vb_ref[...] = pb[:, 2 * WIDTH_B:].astype(BF16)


def _project(x, nw, wat, wb, cos_t, sin_t, qn, kn):
    B, S, D = x.shape
    tm = min(PROJ_TILE, S)
    grid = (B, S // tm)
    row_blk = lambda w: pl.BlockSpec((None, tm, w), lambda b, i: (b, i, 0))
    col_blk = lambda w: pl.BlockSpec((None, w, tm), lambda b, i: (b, 0, i))
    return pl.pallas_call(
        _proj_kernel,
        grid=grid,
        in_specs=[row_blk(D), _const_spec((1, D)), _const_spec((WIDTH_QKV_A, D)), _const_spec((D, WIDTH_QKV_B)),
                  pl.BlockSpec((HEAD_DIM, tm), lambda b, i: (0, i)),
                  pl.BlockSpec((HEAD_DIM, tm), lambda b, i: (0, i)),
                  _const_spec((HEAD_DIM, 1)), _const_spec((HEAD_DIM, 1))],
        out_specs=[col_blk(WIDTH_A), row_blk(WIDTH_KV_A), col_blk(WIDTH_KV_A),
                   row_blk(WIDTH_B), row_blk(WIDTH_B), row_blk(WIDTH_B)],
        out_shape=[jax.ShapeDtypeStruct((B, WIDTH_A, S), BF16),
                   jax.ShapeDtypeStruct((B, S, WIDTH_KV_A), BF16),
                   jax.ShapeDtypeStruct((B, WIDTH_KV_A, S), BF16),
                   jax.ShapeDtypeStruct((B, S, WIDTH_B), BF16),
                   jax.ShapeDtypeStruct((B, S, WIDTH_B), BF16),
                   jax.ShapeDtypeStruct((B, S, WIDTH_B), BF16)],
        compiler_params=_params(2, 40),
        name="proj",
    )(x, nw, wat, wb, cos_t, sin_t, qn, kn)


def _gqa_kernel(qt_ref, k_ref, vt_ref, o_ref, *, tk, n_kv):
    tq = qt_ref.shape[1]
    zeros = jnp.zeros((HEAD_DIM, tq), BF16)
    for pair in range(N_HEADS_A // 2):
        outs = []
        for sub in range(2):
            hh = 2 * pair + sub
            g = hh // GROUP_A
            qh = qt_ref[hh * HEAD_DIM:(hh + 1) * HEAD_DIM, :]
            parts = [zeros] * N_KV_HEADS_A
            parts[g] = qh
            qp = jnp.concatenate(parts, axis=0)

            def body(j, carry, qp=qp):
                m, l, acc = carry
                start = pl.multiple_of(j * tk, tk)
                s = jnp.dot(k_ref[pl.ds(start, tk), :], qp, preferred_element_type=F32)
                m_new = jnp.maximum(m, jnp.max(s, axis=0, keepdims=True))
                alpha = jnp.exp(m - m_new)
                p = jnp.exp(s - m_new)
                l = alpha * l + jnp.sum(p, axis=0, keepdims=True)
                pv = jnp.dot(vt_ref[:, pl.ds(start, tk)], p.astype(BF16), preferred_element_type=F32)
                return m_new, l, alpha * acc + pv

            init = (jnp.full((1, tq), -jnp.inf, F32), jnp.zeros((1, tq), F32),
                    jnp.zeros((WIDTH_KV_A, tq), F32))
            _, l, acc = lax.fori_loop(0, n_kv, body, init)
            outs.append(acc[g * HEAD_DIM:(g + 1) * HEAD_DIM] / l)
        o_pair = jnp.concatenate(outs, axis=0)
        o_ref[:, pair * LANES:(pair + 1) * LANES] = o_pair.T.astype(BF16)


def _gqa_attention(qt, k, vt):
    B, _, S = qt.shape
    tq = min(GQA_TQ, S)
    tk = min(GQA_TK, S)
    kern = functools.partial(_gqa_kernel, tk=tk, n_kv=S // tk)
    return pl.pallas_call(
        kern,
        grid=(B, S // tq),
        in_specs=[pl.BlockSpec((None, WIDTH_A, tq), lambda b, i: (b, 0, i)),
                  pl.BlockSpec((None, S, WIDTH_KV_A), lambda b, i: (b, 0, 0)),
                  pl.BlockSpec((None, WIDTH_KV_A, S), lambda b, i: (b, 0, 0))],
        out_specs=pl.BlockSpec((None, tq, WIDTH_A), lambda b, i: (b, i, 0)),
        out_shape=jax.ShapeDtypeStruct((B, S, WIDTH_A), BF16),
        compiler_params=_params(2, 40),
        name="gqa",
    )(qt, k, vt)


def _na_kernel(q_ref, kp_ref, km_ref, kn_ref, vp_ref, vm_ref, vn_ref, bias_ref, o_ref,
               kband, vband, *, rows, rb):
    i = pl.program_id(1)
    halo = NA_MAX_ROWS * GRID_W
    main = rb * GRID_W
    kband[0:halo, :] = kp_ref[...]
    kband[halo:halo + main, :] = km_ref[...]
    kband[halo + main:, :] = kn_ref[...]
    vband[0:halo, :] = vp_ref[...]
    vband[halo:halo + main, :] = vm_ref[...]
    vband[halo + main:, :] = vn_ref[...]
    lane = lax.broadcasted_iota(jnp.int32, (GRID_W, LANES), 1)
    band_keys = NA_MAX_ROWS * GRID_W

    def row_body(rl, carry):
        r = i * rb + rl
        rs = jnp.clip(r - NA_MAX_ROWS // 2, 0, rows - NA_MAX_ROWS)
        off = rs - r + NA_MAX_ROWS - 1
        base = pl.multiple_of((rs - i * rb + NA_MAX_ROWS) * GRID_W, GRID_W)
        qrow = pl.multiple_of(rl * GRID_W, GRID_W)
        for pair in range(N_HEADS_B // 2):
            cols = slice(pair * LANES, (pair + 1) * LANES)
            qp = q_ref[pl.ds(qrow, GRID_W), cols]
            kp = kband[pl.ds(base, band_keys), cols]
            vp = vband[pl.ds(base, band_keys), cols]
            o_pair = None
            for sub in range(2):
                hh = 2 * pair + sub
                in_head = (lane < HEAD_DIM) if sub == 0 else (lane >= HEAD_DIM)
                qm = jnp.where(in_head, qp, jnp.zeros_like(qp))
                s = lax.dot_general(qm, kp, (((1,), (1,)), ((), ())), preferred_element_type=F32)
                s = s + bias_ref[off * N_HEADS_B + hh]
                m = jnp.max(s, axis=-1, keepdims=True)
                e = jnp.exp(s - m)
                l = jnp.sum(e, axis=-1, keepdims=True)
                o = jnp.dot(e.astype(BF16), vp, preferred_element_type=F32) / l
                o_pair = o if sub == 0 else jnp.where(in_head, o, o_pair)
            o_ref[pl.ds(qrow, GRID_W), cols] = o_pair.astype(BF16)
        return carry

    lax.fori_loop(0, rb, row_body, 0)


def _na_bias_table(rpb):
    c = np.arange(GRID_W)
    col_start = np.clip(c - NA_COLS // 2, 0, GRID_W - NA_COLS)
    in_win = (c[None, :] >= col_start[:, None]) & (c[None, :] < col_start[:, None] + NA_COLS)
    col_idx = np.clip(c[None, :] - c[:, None], -(NA_COLS - 1), NA_COLS - 1) + NA_COLS - 1
    cols = jnp.where(jnp.asarray(in_win)[None, None], rpb[:, :, col_idx], -jnp.inf)
    tabs = jnp.stack([cols[:, o:o + NA_MAX_ROWS] for o in range(NA_MAX_ROWS)])
    tabs = tabs.transpose(0, 1, 3, 2, 4)
    return tabs.reshape(NA_MAX_ROWS * N_HEADS_B, GRID_W, NA_MAX_ROWS * GRID_W).astype(F32)


def _na_attention(qb, kb, vb, bias):
    B, S, W = qb.shape
    rows = S // GRID_W
    assert rows >= NA_MAX_ROWS and rows % NA_MAX_ROWS == 0
    rb = min(NA_BLOCK_ROWS, rows)
    halo = NA_MAX_ROWS * GRID_W
    per = rb // NA_MAX_ROWS
    last = rows // NA_MAX_ROWS - 1
    main_spec = pl.BlockSpec((None, rb * GRID_W, W), lambda b, i: (b, i, 0))
    prev_spec = pl.BlockSpec((None, halo, W), lambda b, i: (b, jnp.maximum(i * per - 1, 0), 0))
    next_spec = pl.BlockSpec((None, halo, W), lambda b, i: (b, jnp.minimum((i + 1) * per, last), 0))
    kern = functools.partial(_na_kernel, rows=rows, rb=rb)
    band = pltpu.VMEM(((rb + 2 * NA_MAX_ROWS) * GRID_W, W), BF16)
    return pl.pallas_call(
        kern,
        grid=(B, rows // rb),
        in_specs=[main_spec, prev_spec, main_spec, next_spec, prev_spec, main_spec, next_spec,
                  _const_spec(bias.shape)],
        out_specs=main_spec,
        out_shape=jax.ShapeDtypeStruct((B, S, W), BF16),
        scratch_shapes=[band, band],
        compiler_params=_params(2, 48),
        name="natten",
    )(qb, kb, kb, kb, vb, vb, vb, bias)


def _merge_kernel(x_ref, oa_ref, ob_ref, nw_ref, wg_ref, bg_ref, wpa_ref, wpb_ref, wout_ref, pmn_ref, o_ref):
    x = x_ref[...]
    h = _rms(x, nw_ref[...]).astype(BF16)
    g = jnp.dot(h, wg_ref[...], preferred_element_type=F32) + bg_ref[...]
    gate = jax.nn.sigmoid(g)
    pa = jnp.dot(oa_ref[...], wpa_ref[...], preferred_element_type=F32)
    pb = jnp.dot(ob_ref[...], wpb_ref[...], preferred_element_type=F32)
    mix = gate[:, :D_MODEL] * pa + gate[:, D_MODEL:] * pb
    mo = jnp.dot(mix.astype(BF16), wout_ref[...], preferred_element_type=F32)
    o_ref[...] = x + _rms(mo, pmn_ref[...])


def _merge(x2, oa2, ob2, nw, wg, bg, wpa, wpb, wout, pmn):
    n, D = x2.shape
    tm = min(PROJ_TILE, n)
    row = lambda w: pl.BlockSpec((tm, w), lambda i: (i, 0))
    return pl.pallas_call(
        _merge_kernel,
        grid=(n // tm,),
        in_specs=[row(D), row(WIDTH_A), row(WIDTH_B), _const_spec((1, D)), _const_spec(wg.shape),
                  _const_spec((1, 2 * D)), _const_spec(wpa.shape), _const_spec(wpb.shape),
                  _const_spec(wout.shape), _const_spec((1, D))],
        out_specs=row(D),
        out_shape=jax.ShapeDtypeStruct((n, D), F32),
        compiler_params=_params(1, 40),
        name="merge",
    )(x2, oa2, ob2, nw, wg, bg, wpa, wpb, wout, pmn)


def _ffn_kernel(xm_ref, xp_ref, xn_ref, nw_ref, wup_ref, cw_ref, cb_ref, wdn_ref, pfn_ref, o_ref,
                u_scr, f_scr, *, tiles_per_seq):
    tm = xm_ref.shape[0]
    j = pl.program_id(0) % tiles_per_seq
    nw = nw_ref[...]
    xm = xm_ref[...]
    hp = _rms(xp_ref[...], nw) * (j > 0).astype(F32)
    hn = _rms(xn_ref[...], nw) * (j < tiles_per_seq - 1).astype(F32)
    h_ext = jnp.concatenate([hp, _rms(xm, nw), hn], axis=0).astype(BF16)
    for c in range(D_FF // FFN_CHUNK):
        conv = []
        for part in range(2):
            lo = part * D_FF + c * FFN_CHUNK
            cols = slice(lo, lo + FFN_CHUNK)
            u_scr[part] = jnp.dot(h_ext, wup_ref[:, cols], preferred_element_type=F32)
            acc = u_scr[part, pl.ds(HALO - 1, tm), :] * cw_ref[0:1, cols] + cb_ref[:, cols]
            acc = acc + u_scr[part, pl.ds(HALO, tm), :] * cw_ref[1:2, cols]
            acc = acc + u_scr[part, pl.ds(HALO + 1, tm), :] * cw_ref[2:3, cols]
            conv.append(acc)
        f = jax.nn.gelu(conv[0], approximate=True) * conv[1]
        f_scr[:, c * FFN_CHUNK:(c + 1) * FFN_CHUNK] = f.astype(BF16)
    y = jnp.dot(f_scr[...], wdn_ref[...], preferred_element_type=F32)
    o_ref[...] = xm + _rms(y, pfn_ref[...])


def _ffn(x2, seq_len, nw, wup, cw, cb, wdn, pfn):
    n, D = x2.shape
    tm = min(PROJ_TILE, seq_len)
    per = tm // HALO
    last = n // HALO - 1
    kern = functools.partial(_ffn_kernel, tiles_per_seq=seq_len // tm)
    return pl.pallas_call(
        kern,
        grid=(n // tm,),
        in_specs=[pl.BlockSpec((tm, D), lambda i: (i, 0)),
                  pl.BlockSpec((HALO, D), lambda i: (jnp.maximum(i * per - 1, 0), 0)),
                  pl.BlockSpec((HALO, D), lambda i: (jnp.minimum((i + 1) * per, last), 0)),
                  _const_spec((1, D)), _const_spec(wup.shape), _const_spec(cw.shape), _const_spec(cb.shape),
                  _const_spec(wdn.shape), _const_spec((1, D))],
        out_specs=pl.BlockSpec((tm, D), lambda i: (i, 0)),
        out_shape=jax.ShapeDtypeStruct((n, D), F32),
        scratch_shapes=[pltpu.VMEM((2, tm + 2 * HALO, FFN_CHUNK), F32), pltpu.VMEM((tm, D_FF), BF16)],
        compiler_params=_params(1, 48),
        name="ffn",
    )(x2, x2, x2, nw, wup, cw, cb, wdn, pfn)


def _rope_tables_t(seq_len):
    t = np.arange(seq_len)
    row = (t // GRID_W).astype(np.float32)
    col = (t % GRID_W).astype(np.float32)
    half = HEAD_DIM // 2
    freqs = (ROPE_THETA ** (-np.arange(0, half, 2, dtype=np.float32) / half)).astype(np.float32)
    ang_r = row[:, None] * freqs[None, :]
    ang_c = col[:, None] * freqs[None, :]
    ang = np.concatenate([ang_r, ang_r, ang_c, ang_c], axis=-1).astype(np.float32)
    return jnp.asarray(np.cos(ang).T, F32), jnp.asarray(np.sin(ang).T, F32)


def _prepare_weights(pre_mix_norm, w_in, b_gate, q_norm, k_norm, rpb, w_proj_a, w_proj_b, w_out,
                     post_mix_norm, pre_ffn_norm, w_up, conv_w, conv_b, w_down, post_ffn_norm):
    row = lambda v: v.reshape(1, -1).astype(F32)
    return dict(
        nw1=row(pre_mix_norm),
        wat=w_in[:, :WIDTH_QKV_A].T.astype(BF16),
        wb=w_in[:, WIDTH_QKV_A:WIDTH_QKV_A + WIDTH_QKV_B].astype(BF16),
        wg=w_in[:, WIDTH_QKV_A + WIDTH_QKV_B:].astype(BF16),
        bg=row(b_gate),
        qn=q_norm.reshape(HEAD_DIM, 1).astype(F32),
        kn=k_norm.reshape(HEAD_DIM, 1).astype(F32),
        bias=_na_bias_table(rpb),
        wpa=w_proj_a.astype(BF16), wpb=w_proj_b.astype(BF16), wout=w_out.astype(BF16),
        pmn=row(post_mix_norm), nw2=row(pre_ffn_norm),
        wup=w_up.astype(BF16), cw=conv_w.astype(F32), cb=row(conv_b),
        wdn=w_down.astype(BF16), pfn=row(post_ffn_norm),
    )


def _encoder_layer(x, w):
    B, S, D = x.shape
    cos_t, sin_t = _rope_tables_t(S)
    qt, k, vt, qb, kb, vb = _project(x, w["nw1"], w["wat"], w["wb"], cos_t, sin_t, w["qn"], w["kn"])
    oa = _gqa_attention(qt, k, vt)
    ob = _na_attention(qb, kb, vb, w["bias"])
    x2 = x.reshape(B * S, D)
    x2 = _merge(x2, oa.reshape(B * S, WIDTH_A), ob.reshape(B * S, WIDTH_B), w["nw1"], w["wg"], w["bg"],
                w["wpa"], w["wpb"], w["wout"], w["pmn"])
    x2 = _ffn(x2, S, w["nw2"], w["wup"], w["cw"], w["cb"], w["wdn"], w["pfn"])
    return x2.reshape(B, S, D)


def kernel(x_prompt, x_sample, pre_mix_norm, w_in, b_gate, q_norm, k_norm, rpb, w_proj_a, w_proj_b, w_out,
           post_mix_norm, pre_ffn_norm, w_up, conv_w, conv_b, w_down, post_ffn_norm):
    layers = [_prepare_weights(pre_mix_norm[l], w_in[l], b_gate[l], q_norm[l], k_norm[l], rpb[l],
                               w_proj_a[l], w_proj_b[l], w_out[l], post_mix_norm[l], pre_ffn_norm[l],
                               w_up[l], conv_w[l], conv_b[l], w_down[l], post_ffn_norm[l])
              for l in range(w_in.shape[0])]

    def run_trunk(x):
        for w in layers:
            x = _encoder_layer(x, w)
        return x

    return run_trunk(x_prompt), run_trunk(x_sample)
```

```python
import functools

import numpy as np
import jax
import jax.numpy as jnp
from jax import lax
from jax.experimental import pallas as pl
from jax.experimental.pallas import tpu as pltpu

D_MODEL = 1024
GRID_W = 64
HEAD_DIM = 64
N_HEADS_A = 8
N_KV_HEADS_A = 2
N_HEADS_B = 8
WIDTH_A = N_HEADS_A * HEAD_DIM
WIDTH_KV_A = N_KV_HEADS_A * HEAD_DIM
WIDTH_B = N_HEADS_B * HEAD_DIM
NA_MAX_ROWS = 8
NA_COLS = 16
ROPE_THETA = 10000.0
D_FF = 2816
CONV_WIDTH = 3
EPS = 1e-6
SCALE = HEAD_DIM ** -0.5
GROUP_A = N_HEADS_A // N_KV_HEADS_A
WIDTH_QKV_A = WIDTH_A + 2 * WIDTH_KV_A
WIDTH_QKV_B = 3 * WIDTH_B

F32 = jnp.float32
BF16 = jnp.bfloat16
MIB = 1024 * 1024

SUBLANES = 8
LANES = 128

PROJ_TILE = 512
GQA_TQ = 256
GQA_TK = 512
GQA_LOOKAHEAD = 2
NA_BLOCK_ROWS = 32
FFN_CHUNK = 256
HALO = SUBLANES


def _const_spec(shape):
    n = len(shape)
    return pl.BlockSpec(shape, lambda *_: (0,) * n, pipeline_mode=pl.Buffered(1))


def _params(n_axes, vmem_mib):
    return pltpu.CompilerParams(dimension_semantics=("parallel",) * n_axes,
                                vmem_limit_bytes=vmem_mib * MIB)


def _rms(x, w):
    return x * lax.rsqrt(jnp.mean(x * x, axis=-1, keepdims=True) + EPS) * w


def _proj_kernel(x_ref, nw_ref, wat_ref, wb_ref, cos_ref, sin_ref, qn_ref, kn_ref,
                 qt_ref, k_ref, vt_ref, qb_ref, kb_ref, vb_ref):
    h = _rms(x_ref[...], nw_ref[...]).astype(BF16)
    pa = lax.dot_general(wat_ref[...], h, (((1,), (1,)), ((), ())), preferred_element_type=F32)
    cos = cos_ref[...]
    sin = sin_ref[...]

    def norm_rope(t, w):
        t = t * lax.rsqrt(jnp.mean(t * t, axis=0, keepdims=True) + EPS) * w
        q4 = HEAD_DIM // 4
        rot = jnp.concatenate([-t[q4:2 * q4], t[0:q4], -t[3 * q4:4 * q4], t[2 * q4:3 * q4]], axis=0)
        return t * cos + rot * sin

    qn = qn_ref[...]
    kn = kn_ref[...]
    for hh in range(N_HEADS_A):
        lo = hh * HEAD_DIM
        qt_ref[lo:lo + HEAD_DIM, :] = (norm_rope(pa[lo:lo + HEAD_DIM], qn) * SCALE).astype(BF16)
    kt = jnp.concatenate(
        [norm_rope(pa[WIDTH_A + g * HEAD_DIM:WIDTH_A + (g + 1) * HEAD_DIM], kn) for g in range(N_KV_HEADS_A)],
        axis=0)
    k_ref[...] = kt.T.astype(BF16)
    vt_ref[...] = pa[WIDTH_A + WIDTH_KV_A:].astype(BF16)

    pb = jnp.dot(h, wb_ref[...], preferred_element_type=F32)
    qb_ref[...] = (pb[:, :WIDTH_B] * SCALE).astype(BF16)
    kb_ref[...] = pb[:, WIDTH_B:2 * WIDTH_B].astype(BF16)
    vb_ref[...] = pb[:, 2 * WIDTH_B:].astype(BF16)


def _project(x, nw, wat, wb, cos_t, sin_t, qn, kn):
    B, S, D = x.shape
    tm = min(PROJ_TILE, S)
    grid = (B, S // tm)
    row_blk = lambda w: pl.BlockSpec((None, tm, w), lambda b, i: (b, i, 0))
    col_blk = lambda w: pl.BlockSpec((None, w, tm), lambda b, i: (b, 0, i))
    return pl.pallas_call(
        _proj_kernel,
        grid=grid,
        in_specs=[row_blk(D), _const_spec((1, D)), _const_spec((WIDTH_QKV_A, D)), _const_spec((D, WIDTH_QKV_B)),
                  pl.BlockSpec((HEAD_DIM, tm), lambda b, i: (0, i)),
                  pl.BlockSpec((HEAD_DIM, tm), lambda b, i: (0, i)),
                  _const_spec((HEAD_DIM, 1)), _const_spec((HEAD_DIM, 1))],
        out_specs=[col_blk(WIDTH_A), row_blk(WIDTH_KV_A), col_blk(WIDTH_KV_A),
                   row_blk(WIDTH_B), row_blk(WIDTH_B), row_blk(WIDTH_B)],
        out_shape=[jax.ShapeDtypeStruct((B, WIDTH_A, S), BF16),
                   jax.ShapeDtypeStruct((B, S, WIDTH_KV_A), BF16),
                   jax.ShapeDtypeStruct((B, WIDTH_KV_A, S), BF16),
                   jax.ShapeDtypeStruct((B, S, WIDTH_B), BF16),
                   jax.ShapeDtypeStruct((B, S, WIDTH_B), BF16),
                   jax.ShapeDtypeStruct((B, S, WIDTH_B), BF16)],
        compiler_params=_params(2, 40),
        name="proj",
    )(x, nw, wat, wb, cos_t, sin_t, qn, kn)


def _gqa_kernel(qt_ref, k_ref, vt_ref, o_ref, qp_ref, m_ref, l_ref, acc_ref, *, tk, n_kv):
    tq = qt_ref.shape[1]
    zeros = jnp.zeros((HEAD_DIM, tq), BF16)
    for hh in range(N_HEADS_A):
        g = hh // GROUP_A
        parts = [zeros] * N_KV_HEADS_A
        parts[g] = qt_ref[hh * HEAD_DIM:(hh + 1) * HEAD_DIM, :]
        qp_ref[hh] = jnp.concatenate(parts, axis=0)
    m_ref[...] = jnp.full(m_ref.shape, -jnp.inf, F32)
    l_ref[...] = jnp.zeros(l_ref.shape, F32)
    acc_ref[...] = jnp.zeros(acc_ref.shape, F32)

    def body(j, carry):
        start = pl.multiple_of(j * tk, tk)
        kc = k_ref[pl.ds(start, tk), :]
        def scores(hh):
            return jnp.dot(kc, qp_ref[hh], preferred_element_type=F32)

        pending = [scores(hh) for hh in range(GQA_LOOKAHEAD)]
        for hh in range(N_HEADS_A):
            g = hh // GROUP_A
            s = pending.pop(0)
            if hh + GQA_LOOKAHEAD < N_HEADS_A:
                pending.append(scores(hh + GQA_LOOKAHEAD))
            m_old = m_ref[hh]
            m_new = jnp.maximum(m_old, jnp.max(s, axis=0, keepdims=True))
            alpha = jnp.exp(m_old - m_new)
            p = jnp.exp(s - m_new)
            l_ref[hh] = alpha * l_ref[hh] + jnp.sum(p, axis=0, keepdims=True)
            m_ref[hh] = m_new
            vc = vt_ref[g * HEAD_DIM:(g + 1) * HEAD_DIM, pl.ds(start, tk)]
            pv = jnp.dot(vc, p.astype(BF16), preferred_element_type=F32)
            acc_ref[hh] = alpha * acc_ref[hh] + pv
        return carry

    lax.fori_loop(0, n_kv, body, 0)
    for pair in range(N_HEADS_A // 2):
        o_pair = jnp.concatenate([acc_ref[hh] / l_ref[hh] for hh in (2 * pair, 2 * pair + 1)], axis=0)
        o_ref[:, pair * LANES:(pair + 1) * LANES] = o_pair.T.astype(BF16)


def _gqa_attention(qt, k, vt):
    B, _, S = qt.shape
    tq = min(GQA_TQ, S)
    tk = min(GQA_TK, S)
    kern = functools.partial(_gqa_kernel, tk=tk, n_kv=S // tk)
    return pl.pallas_call(
        kern,
        grid=(B, S // tq),
        in_specs=[pl.BlockSpec((None, WIDTH_A, tq), lambda b, i: (b, 0, i)),
                  pl.BlockSpec((None, S, WIDTH_KV_A), lambda b, i: (b, 0, 0)),
                  pl.BlockSpec((None, WIDTH_KV_A, S), lambda b, i: (b, 0, 0))],
        out_specs=pl.BlockSpec((None, tq, WIDTH_A), lambda b, i: (b, i, 0)),
        out_shape=jax.ShapeDtypeStruct((B, S, WIDTH_A), BF16),
        scratch_shapes=[pltpu.VMEM((N_HEADS_A, WIDTH_KV_A, tq), BF16),
                        pltpu.VMEM((N_HEADS_A, 1, tq), F32),
                        pltpu.VMEM((N_HEADS_A, 1, tq), F32),
                        pltpu.VMEM((N_HEADS_A, HEAD_DIM, tq), F32)],
        compiler_params=_params(2, 40),
        name="gqa",
    )(qt, k, vt)


def _na_kernel(q_ref, kp_ref, km_ref, kn_ref, vp_ref, vm_ref, vn_ref, bias_ref, o_ref,
               kband, vband, *, rows, rb):
    i = pl.program_id(1)
    halo = NA_MAX_ROWS * GRID_W
    main = rb * GRID_W
    kband[0:halo, :] = kp_ref[...]
    kband[halo:halo + main, :] = km_ref[...]
    kband[halo + main:, :] = kn_ref[...]
    vband[0:halo, :] = vp_ref[...]
    vband[halo:halo + main, :] = vm_ref[...]
    vband[halo + main:, :] = vn_ref[...]
    lane = lax.broadcasted_iota(jnp.int32, (GRID_W, LANES), 1)
    band_keys = NA_MAX_ROWS * GRID_W

    def row_body(rl, carry):
        r = i * rb + rl
        rs = jnp.clip(r - NA_MAX_ROWS // 2, 0, rows - NA_MAX_ROWS)
        off = rs - r + NA_MAX_ROWS - 1
        base = pl.multiple_of((rs - i * rb + NA_MAX_ROWS) * GRID_W, GRID_W)
        qrow = pl.multiple_of(rl * GRID_W, GRID_W)
        for pair in range(N_HEADS_B // 2):
            cols = slice(pair * LANES, (pair + 1) * LANES)
            qp = q_ref[pl.ds(qrow, GRID_W), cols]
            kp = kband[pl.ds(base, band_keys), cols]
            vp = vband[pl.ds(base, band_keys), cols]
            o_pair = None
            for sub in range(2):
                hh = 2 * pair + sub
                in_head = (lane < HEAD_DIM) if sub == 0 else (lane >= HEAD_DIM)
                qm = jnp.where(in_head, qp, jnp.zeros_like(qp))
                s = lax.dot_general(qm, kp, (((1,), (1,)), ((), ())), preferred_element_type=F32)
                s = s + bias_ref[off * N_HEADS_B + hh]
                m = jnp.max(s, axis=-1, keepdims=True)
                e = jnp.exp(s - m)
                l = jnp.sum(e, axis=-1, keepdims=True)
                o = jnp.dot(e.astype(BF16), vp, preferred_element_type=F32) / l
                o_pair = o if sub == 0 else jnp.where(in_head, o, o_pair)
            o_ref[pl.ds(qrow, GRID_W), cols] = o_pair.astype(BF16)
        return carry

    lax.fori_loop(0, rb, row_body, 0)


def _na_bias_table(rpb):
    c = np.arange(GRID_W)
    col_start = np.clip(c - NA_COLS // 2, 0, GRID_W - NA_COLS)
    in_win = (c[None, :] >= col_start[:, None]) & (c[None, :] < col_start[:, None] + NA_COLS)
    col_idx = np.clip(c[None, :] - c[:, None], -(NA_COLS - 1), NA_COLS - 1) + NA_COLS - 1
    cols = jnp.where(jnp.asarray(in_win)[None, None], rpb[:, :, col_idx], -jnp.inf)
    tabs = jnp.stack([cols[:, o:o + NA_MAX_ROWS] for o in range(NA_MAX_ROWS)])
    tabs = tabs.transpose(0, 1, 3, 2, 4)
    return tabs.reshape(NA_MAX_ROWS * N_HEADS_B, GRID_W, NA_MAX_ROWS * GRID_W).astype(F32)


def _na_attention(qb, kb, vb, bias):
    B, S, W = qb.shape
    rows = S // GRID_W
    assert rows >= NA_MAX_ROWS and rows % NA_MAX_ROWS == 0
    rb = min(NA_BLOCK_ROWS, rows)
    halo = NA_MAX_ROWS * GRID_W
    per = rb // NA_MAX_ROWS
    last = rows // NA_MAX_ROWS - 1
    main_spec = pl.BlockSpec((None, rb * GRID_W, W), lambda b, i: (b, i, 0))
    prev_spec = pl.BlockSpec((None, halo, W), lambda b, i: (b, jnp.maximum(i * per - 1, 0), 0))
    next_spec = pl.BlockSpec((None, halo, W), lambda b, i: (b, jnp.minimum((i + 1) * per, last), 0))
    kern = functools.partial(_na_kernel, rows=rows, rb=rb)
    band = pltpu.VMEM(((rb + 2 * NA_MAX_ROWS) * GRID_W, W), BF16)
    return pl.pallas_call(
        kern,
        grid=(B, rows // rb),
        in_specs=[main_spec, prev_spec, main_spec, next_spec, prev_spec, main_spec, next_spec,
                  _const_spec(bias.shape)],
        out_specs=main_spec,
        out_shape=jax.ShapeDtypeStruct((B, S, W), BF16),
        scratch_shapes=[band, band],
        compiler_params=_params(2, 48),
        name="natten",
    )(qb, kb, kb, kb, vb, vb, vb, bias)


def _merge_kernel(x_ref, oa_ref, ob_ref, nw_ref, wg_ref, bg_ref, wpa_ref, wpb_ref, wout_ref, pmn_ref, o_ref):
    x = x_ref[...]
    h = _rms(x, nw_ref[...]).astype(BF16)
    g = jnp.dot(h, wg_ref[...], preferred_element_type=F32) + bg_ref[...]
    gate = jax.nn.sigmoid(g)
    pa = jnp.dot(oa_ref[...], wpa_ref[...], preferred_element_type=F32)
    pb = jnp.dot(ob_ref[...], wpb_ref[...], preferred_element_type=F32)
    mix = gate[:, :D_MODEL] * pa + gate[:, D_MODEL:] * pb
    mo = jnp.dot(mix.astype(BF16), wout_ref[...], preferred_element_type=F32)
    o_ref[...] = x + _rms(mo, pmn_ref[...])


def _merge(x2, oa2, ob2, nw, wg, bg, wpa, wpb, wout, pmn):
    n, D = x2.shape
    tm = min(PROJ_TILE, n)
    row = lambda w: pl.BlockSpec((tm, w), lambda i: (i, 0))
    return pl.pallas_call(
        _merge_kernel,
        grid=(n // tm,),
        in_specs=[row(D), row(WIDTH_A), row(WIDTH_B), _const_spec((1, D)), _const_spec(wg.shape),
                  _const_spec((1, 2 * D)), _const_spec(wpa.shape), _const_spec(wpb.shape),
                  _const_spec(wout.shape), _const_spec((1, D))],
        out_specs=row(D),
        out_shape=jax.ShapeDtypeStruct((n, D), F32),
        compiler_params=_params(1, 40),
        name="merge",
    )(x2, oa2, ob2, nw, wg, bg, wpa, wpb, wout, pmn)


def _ffn_kernel(xm_ref, xp_ref, xn_ref, nw_ref, wup_ref, cw_ref, cb_ref, wdn_ref, pfn_ref, o_ref,
                u_scr, f_scr, *, tiles_per_seq):
    tm = xm_ref.shape[0]
    j = pl.program_id(0) % tiles_per_seq
    nw = nw_ref[...]
    xm = xm_ref[...]
    hp = _rms(xp_ref[...], nw) * (j > 0).astype(F32)
    hn = _rms(xn_ref[...], nw) * (j < tiles_per_seq - 1).astype(F32)
    h_ext = jnp.concatenate([hp, _rms(xm, nw), hn], axis=0).astype(BF16)
    for c in range(D_FF // FFN_CHUNK):
        conv = []
        for part in range(2):
            lo = part * D_FF + c * FFN_CHUNK
            cols = slice(lo, lo + FFN_CHUNK)
            u_scr[part] = jnp.dot(h_ext, wup_ref[:, cols], preferred_element_type=F32)
            acc = u_scr[part, pl.ds(HALO - 1, tm), :] * cw_ref[0:1, cols] + cb_ref[:, cols]
            acc = acc + u_scr[part, pl.ds(HALO, tm), :] * cw_ref[1:2, cols]
            acc = acc + u_scr[part, pl.ds(HALO + 1, tm), :] * cw_ref[2:3, cols]
            conv.append(acc)
        f = jax.nn.gelu(conv[0], approximate=True) * conv[1]
        f_scr[:, c * FFN_CHUNK:(c + 1) * FFN_CHUNK] = f.astype(BF16)
    y = jnp.dot(f_scr[...], wdn_ref[...], preferred_element_type=F32)
    o_ref[...] = xm + _rms(y, pfn_ref[...])


def _ffn(x2, seq_len, nw, wup, cw, cb, wdn, pfn):
    n, D = x2.shape
    tm = min(PROJ_TILE, seq_len)
    per = tm // HALO
    last = n // HALO - 1
    kern = functools.partial(_ffn_kernel, tiles_per_seq=seq_len // tm)
    return pl.pallas_call(
        kern,
        grid=(n // tm,),
        in_specs=[pl.BlockSpec((tm, D), lambda i: (i, 0)),
                  pl.BlockSpec((HALO, D), lambda i: (jnp.maximum(i * per - 1, 0), 0)),
                  pl.BlockSpec((HALO, D), lambda i: (jnp.minimum((i + 1) * per, last), 0)),
                  _const_spec((1, D)), _const_spec(wup.shape), _const_spec(cw.shape), _const_spec(cb.shape),
                  _const_spec(wdn.shape), _const_spec((1, D))],
        out_specs=pl.BlockSpec((tm, D), lambda i: (i, 0)),
        out_shape=jax.ShapeDtypeStruct((n, D), F32),
        scratch_shapes=[pltpu.VMEM((2, tm + 2 * HALO, FFN_CHUNK), F32), pltpu.VMEM((tm, D_FF), BF16)],
        compiler_params=_params(1, 48),
        name="ffn",
    )(x2, x2, x2, nw, wup, cw, cb, wdn, pfn)


def _rope_tables_t(seq_len):
    t = np.arange(seq_len)
    row = (t // GRID_W).astype(np.float32)
    col = (t % GRID_W).astype(np.float32)
    half = HEAD_DIM // 2
    freqs = (ROPE_THETA ** (-np.arange(0, half, 2, dtype=np.float32) / half)).astype(np.float32)
    ang_r = row[:, None] * freqs[None, :]
    ang_c = col[:, None] * freqs[None, :]
    ang = np.concatenate([ang_r, ang_r, ang_c, ang_c], axis=-1).astype(np.float32)
    return jnp.asarray(np.cos(ang).T, F32), jnp.asarray(np.sin(ang).T, F32)


def _prepare_weights(pre_mix_norm, w_in, b_gate, q_norm, k_norm, rpb, w_proj_a, w_proj_b, w_out,
                     post_mix_norm, pre_ffn_norm, w_up, conv_w, conv_b, w_down, post_ffn_norm):
    row = lambda v: v.reshape(1, -1).astype(F32)
    return dict(
        nw1=row(pre_mix_norm),
        wat=w_in[:, :WIDTH_QKV_A].T.astype(BF16),
        wb=w_in[:, WIDTH_QKV_A:WIDTH_QKV_A + WIDTH_QKV_B].astype(BF16),
        wg=w_in[:, WIDTH_QKV_A + WIDTH_QKV_B:].astype(BF16),
        bg=row(b_gate),
        qn=q_norm.reshape(HEAD_DIM, 1).astype(F32),
        kn=k_norm.reshape(HEAD_DIM, 1).astype(F32),
        bias=_na_bias_table(rpb),
        wpa=w_proj_a.astype(BF16), wpb=w_proj_b.astype(BF16), wout=w_out.astype(BF16),
        pmn=row(post_mix_norm), nw2=row(pre_ffn_norm),
        wup=w_up.astype(BF16), cw=conv_w.astype(F32), cb=row(conv_b),
        wdn=w_down.astype(BF16), pfn=row(post_ffn_norm),
    )


def _encoder_layer(x, w):
    B, S, D = x.shape
    cos_t, sin_t = _rope_tables_t(S)
    qt, k, vt, qb, kb, vb = _project(x, w["nw1"], w["wat"], w["wb"], cos_t, sin_t, w["qn"], w["kn"])
    oa = _gqa_attention(qt, k, vt)
    ob = _na_attention(qb, kb, vb, w["bias"])
    x2 = x.reshape(B * S, D)
    x2 = _merge(x2, oa.reshape(B * S, WIDTH_A), ob.reshape(B * S, WIDTH_B), w["nw1"], w["wg"], w["bg"],
                w["wpa"], w["wpb"], w["wout"], w["pmn"])
    x2 = _ffn(x2, S, w["nw2"], w["wup"], w["cw"], w["cb"], w["wdn"], w["pfn"])
    return x2.reshape(B, S, D)


def kernel(x_prompt, x_sample, pre_mix_norm, w_in, b_gate, q_norm, k_norm, rpb, w_proj_a, w_proj_b, w_out,
           post_mix_norm, pre_ffn_norm, w_up, conv_w, conv_b, w_down, post_ffn_norm):
    layers = [_prepare_weights(pre_mix_norm[l], w_in[l], b_gate[l], q_norm[l], k_norm[l], rpb[l],
                               w_proj_a[l], w_proj_b[l], w_out[l], post_mix_norm[l], pre_ffn_norm[l],
                               w_up[l], conv_w[l], conv_b[l], w_down[l], post_ffn_norm[l])
              for l in range(w_in.shape[0])]

    def run_trunk(x):
        for w in layers:
            x = _encoder_layer(x, w)
        return x

    return run_trunk(x_prompt), run_trunk(x_sample)
```

```python
import functools

import numpy as np
import jax
import jax.numpy as jnp
from jax import lax
from jax.experimental import pallas as pl
from jax.experimental.pallas import tpu as pltpu

D_MODEL = 1024
GRID_W = 64
HEAD_DIM = 64
N_HEADS_A = 8
N_KV_HEADS_A = 2
N_HEADS_B = 8
WIDTH_A = N_HEADS_A * HEAD_DIM
WIDTH_KV_A = N_KV_HEADS_A * HEAD_DIM
WIDTH_B = N_HEADS_B * HEAD_DIM
NA_MAX_ROWS = 8
NA_COLS = 16
ROPE_THETA = 10000.0
D_FF = 2816
CONV_WIDTH = 3
EPS = 1e-6
SCALE = HEAD_DIM ** -0.5
GROUP_A = N_HEADS_A // N_KV_HEADS_A
WIDTH_QKV_A = WIDTH_A + 2 * WIDTH_KV_A
WIDTH_QKV_B = 3 * WIDTH_B

F32 = jnp.float32
BF16 = jnp.bfloat16
MIB = 1024 * 1024

SUBLANES = 8
LANES = 128

PROJ_TILE = 512
GQA_TQ = 256
GQA_TK = 512
NA_BLOCK_ROWS = 32
NA_GROUP_ROWS = 4
NA_BAND_ROWS = 12
FFN_CHUNK = 256
HALO = SUBLANES


def _const_spec(shape):
    n = len(shape)
    return pl.BlockSpec(shape, lambda *_: (0,) * n, pipeline_mode=pl.Buffered(1))


def _params(n_axes, vmem_mib):
    return pltpu.CompilerParams(dimension_semantics=("parallel",) * n_axes,
                                vmem_limit_bytes=vmem_mib * MIB)


def _rms(x, w):
    return x * lax.rsqrt(jnp.mean(x * x, axis=-1, keepdims=True) + EPS) * w


def _proj_kernel(x_ref, nw_ref, wt_ref, wkb_ref, cos_ref, sin_ref, qn_ref, kn_ref,
                 qt_ref, k_ref, vt_ref, qbt_ref, kb_ref, vbt_ref):
    h = _rms(x_ref[...], nw_ref[...]).astype(BF16)
    pt = lax.dot_general(wt_ref[...], h, (((1,), (1,)), ((), ())), preferred_element_type=F32)
    cos = cos_ref[...]
    sin = sin_ref[...]

    def norm_rope(t, w):
        t = t * lax.rsqrt(jnp.mean(t * t, axis=0, keepdims=True) + EPS) * w
        q4 = HEAD_DIM // 4
        rot = jnp.concatenate([-t[q4:2 * q4], t[0:q4], -t[3 * q4:4 * q4], t[2 * q4:3 * q4]], axis=0)
        return t * cos + rot * sin

    qn = qn_ref[...]
    kn = kn_ref[...]
    for hh in range(N_HEADS_A):
        lo = hh * HEAD_DIM
        qt_ref[lo:lo + HEAD_DIM, :] = (norm_rope(pt[lo:lo + HEAD_DIM], qn) * SCALE).astype(BF16)
    kt = jnp.concatenate(
        [norm_rope(pt[WIDTH_A + g * HEAD_DIM:WIDTH_A + (g + 1) * HEAD_DIM], kn) for g in range(N_KV_HEADS_A)],
        axis=0)
    k_ref[...] = kt.T.astype(BF16)
    vt_ref[...] = pt[WIDTH_A + WIDTH_KV_A:WIDTH_QKV_A].astype(BF16)
    qbt_ref[...] = (pt[WIDTH_QKV_A:WIDTH_QKV_A + WIDTH_B] * SCALE).astype(BF16)
    vbt_ref[...] = pt[WIDTH_QKV_A + WIDTH_B:].astype(BF16)
    kb_ref[...] = jnp.dot(h, wkb_ref[...], preferred_element_type=F32).astype(BF16)


def _project(x, nw, wt, wkb, cos_t, sin_t, qn, kn):
    B, S, D = x.shape
    tm = min(PROJ_TILE, S)
    grid = (B, S // tm)
    row_blk = lambda w: pl.BlockSpec((None, tm, w), lambda b, i: (b, i, 0))
    col_blk = lambda w: pl.BlockSpec((None, w, tm), lambda b, i: (b, 0, i))
    return pl.pallas_call(
        _proj_kernel,
        grid=grid,
        in_specs=[row_blk(D), _const_spec((1, D)), _const_spec(wt.shape), _const_spec(wkb.shape),
                  pl.BlockSpec((HEAD_DIM, tm), lambda b, i: (0, i)),
                  pl.BlockSpec((HEAD_DIM, tm), lambda b, i: (0, i)),
                  _const_spec((HEAD_DIM, 1)), _const_spec((HEAD_DIM, 1))],
        out_specs=[col_blk(WIDTH_A), row_blk(WIDTH_KV_A), col_blk(WIDTH_KV_A),
                   col_blk(WIDTH_B), row_blk(WIDTH_B), col_blk(WIDTH_B)],
        out_shape=[jax.ShapeDtypeStruct((B, WIDTH_A, S), BF16),
                   jax.ShapeDtypeStruct((B, S, WIDTH_KV_A), BF16),
                   jax.ShapeDtypeStruct((B, WIDTH_KV_A, S), BF16),
                   jax.ShapeDtypeStruct((B, WIDTH_B, S), BF16),
                   jax.ShapeDtypeStruct((B, S, WIDTH_B), BF16),
                   jax.ShapeDtypeStruct((B, WIDTH_B, S), BF16)],
        compiler_params=_params(2, 40),
        name="proj",
    )(x, nw, wt, wkb, cos_t, sin_t, qn, kn)


def _gqa_kernel(qt_ref, k_ref, vt_ref, o_ref, qp_ref, m_ref, l_ref, acc_ref, s_scr, *, tk, n_kv):
    tq = qt_ref.shape[1]
    zeros = jnp.zeros((HEAD_DIM, tq), BF16)
    for hh in range(N_HEADS_A):
        g = hh // GROUP_A
        parts = [zeros] * N_KV_HEADS_A
        parts[g] = qt_ref[hh * HEAD_DIM:(hh + 1) * HEAD_DIM, :]
        qp_ref[hh] = jnp.concatenate(parts, axis=0)
    m_ref[...] = jnp.full(m_ref.shape, -jnp.inf, F32)
    l_ref[...] = jnp.zeros(l_ref.shape, F32)
    acc_ref[...] = jnp.zeros(acc_ref.shape, F32)

    def scores(j, hh):
        kc = k_ref[pl.ds(pl.multiple_of(j * tk, tk), tk), :]
        return jnp.dot(kc, qp_ref[hh], preferred_element_type=F32)

    s_scr[0] = scores(0, 0)

    def body(j, carry):
        start = pl.multiple_of(j * tk, tk)
        j_next = jnp.minimum(j + 1, n_kv - 1)
        for hh in range(N_HEADS_A):
            g = hh // GROUP_A
            nxt = (j, hh + 1) if hh + 1 < N_HEADS_A else (j_next, 0)
            s_scr[(hh + 1) % 2] = scores(*nxt)
            s = s_scr[hh % 2]
            m_old = m_ref[hh]
            m_new = jnp.maximum(m_old, jnp.max(s, axis=0, keepdims=True))
            alpha = jnp.exp(m_old - m_new)
            p = jnp.exp(s - m_new)
            l_ref[hh] = alpha * l_ref[hh] + jnp.sum(p, axis=0, keepdims=True)
            m_ref[hh] = m_new
            vc = vt_ref[g * HEAD_DIM:(g + 1) * HEAD_DIM, pl.ds(start, tk)]
            pv = jnp.dot(vc, p.astype(BF16), preferred_element_type=F32)
            acc_ref[hh] = alpha * acc_ref[hh] + pv
        return carry

    lax.fori_loop(0, n_kv, body, 0)
    for pair in range(N_HEADS_A // 2):
        o_pair = jnp.concatenate([acc_ref[hh] / l_ref[hh] for hh in (2 * pair, 2 * pair + 1)], axis=0)
        o_ref[:, pair * LANES:(pair + 1) * LANES] = o_pair.T.astype(BF16)


def _gqa_attention(qt, k, vt):
    B, _, S = qt.shape
    tq = min(GQA_TQ, S)
    tk = min(GQA_TK, S)
    kern = functools.partial(_gqa_kernel, tk=tk, n_kv=S // tk)
    return pl.pallas_call(
        kern,
        grid=(B, S // tq),
        in_specs=[pl.BlockSpec((None, WIDTH_A, tq), lambda b, i: (b, 0, i)),
                  pl.BlockSpec((None, S, WIDTH_KV_A), lambda b, i: (b, 0, 0)),
                  pl.BlockSpec((None, WIDTH_KV_A, S), lambda b, i: (b, 0, 0))],
        out_specs=pl.BlockSpec((None, tq, WIDTH_A), lambda b, i: (b, i, 0)),
        out_shape=jax.ShapeDtypeStruct((B, S, WIDTH_A), BF16),
        scratch_shapes=[pltpu.VMEM((N_HEADS_A, WIDTH_KV_A, tq), BF16),
                        pltpu.VMEM((N_HEADS_A, 1, tq), F32),
                        pltpu.VMEM((N_HEADS_A, 1, tq), F32),
                        pltpu.VMEM((N_HEADS_A, HEAD_DIM, tq), F32),
                        pltpu.VMEM((2, tk, tq), F32)],
        compiler_params=_params(2, 40),
        name="gqa",
    )(qt, k, vt)


def _na_kernel(qt_ref, kp_ref, km_ref, kn_ref, vp_ref, vm_ref, vn_ref, bias_ref, o_ref,
               kband, vband, s_scr, *, rows, rb):
    i = pl.program_id(1)
    halo = NA_MAX_ROWS * GRID_W
    main = rb * GRID_W
    kband[0:halo, :] = kp_ref[...]
    kband[halo:halo + main, :] = km_ref[...]
    kband[halo + main:, :] = kn_ref[...]
    vband[:, 0:halo] = vp_ref[...]
    vband[:, halo:halo + main] = vm_ref[...]
    vband[:, halo + main:] = vn_ref[...]
    nq = NA_GROUP_ROWS * GRID_W
    nk = NA_BAND_ROWS * GRID_W
    zeros = jnp.zeros((HEAD_DIM, nq), BF16)
    n_groups = rows // NA_GROUP_ROWS

    n_local = rb // NA_GROUP_ROWS

    def geometry(gl):
        grp = i * n_local + gl
        band_start = jnp.clip(grp * NA_GROUP_ROWS - NA_MAX_ROWS // 2, 0, rows - NA_BAND_ROWS)
        base = pl.multiple_of((band_start - i * rb + NA_MAX_ROWS) * GRID_W, nq)
        qcol = pl.multiple_of(gl * nq, nq)
        pat = jnp.where(grp == 0, 0, jnp.where(grp == n_groups - 1, 2, 1))
        return base, qcol, pat

    def scores(geo, hh):
        base, qcol, _ = geo
        pair, sub = divmod(hh, 2)
        qh = qt_ref[hh * HEAD_DIM:(hh + 1) * HEAD_DIM, pl.ds(qcol, nq)]
        qp = jnp.concatenate([qh, zeros] if sub == 0 else [zeros, qh], axis=0)
        kb = kband[pl.ds(base, nk), pair * LANES:(pair + 1) * LANES]
        return jnp.dot(kb, qp, preferred_element_type=F32)

    s_scr[0] = scores(geometry(0), 0)

    def group_body(gl, carry):
        geo = geometry(gl)
        geo_next = geometry(jnp.minimum(gl + 1, n_local - 1))
        base, qcol, pat = geo
        outs = []
        for hh in range(N_HEADS_B):
            nxt = (geo, hh + 1) if hh + 1 < N_HEADS_B else (geo_next, 0)
            s_scr[(hh + 1) % 2] = scores(*nxt)
            s = s_scr[hh % 2] + bias_ref[pat * N_HEADS_B + hh]
            m = jnp.max(s, axis=0, keepdims=True)
            e = jnp.exp(s - m)
            l = jnp.sum(e, axis=0, keepdims=True)
            vt = vband[hh * HEAD_DIM:(hh + 1) * HEAD_DIM, pl.ds(base, nk)]
            outs.append(jnp.dot(vt, e.astype(BF16), preferred_element_type=F32) / l)
            if hh % 2 == 1:
                o_pair = jnp.concatenate(outs, axis=0)
                o_ref[pl.ds(qcol, nq), (hh // 2) * LANES:(hh // 2 + 1) * LANES] = o_pair.T.astype(BF16)
                outs = []
        return carry

    lax.fori_loop(0, n_local, group_body, 0)


def _na_bias_table(rpb):
    c = np.arange(GRID_W)
    col_start = np.clip(c - NA_COLS // 2, 0, GRID_W - NA_COLS)
    in_win = (c[None, :] >= col_start[:, None]) & (c[None, :] < col_start[:, None] + NA_COLS)
    col_idx = np.clip(c[None, :] - c[:, None], -(NA_COLS - 1), NA_COLS - 1) + NA_COLS - 1
    i = np.arange(NA_GROUP_ROWS)
    k = np.arange(NA_BAND_ROWS)
    q_pos = np.stack([i, NA_MAX_ROWS // 2 + i, NA_MAX_ROWS + i])
    w_start = np.stack([0 * i, i, NA_MAX_ROWS // 2 + 0 * i])
    valid = (k[None, None, :] >= w_start[:, :, None]) & (k[None, None, :] < w_start[:, :, None] + NA_MAX_ROWS)
    rel = np.clip(k[None, None, :] - q_pos[:, :, None] + NA_MAX_ROWS - 1, 0, 2 * NA_MAX_ROWS - 2)
    vals = rpb[:, rel][..., col_idx]
    ok = jnp.asarray(valid[None, :, :, :, None, None] & in_win[None, None, None, None])
    tab = jnp.where(ok, vals, -jnp.inf)
    tab = tab.transpose(1, 0, 3, 5, 2, 4)
    return tab.reshape(3 * N_HEADS_B, NA_BAND_ROWS * GRID_W, NA_GROUP_ROWS * GRID_W).astype(F32)


def _na_attention(qbt, kb, vbt, bias):
    B, S, W = kb.shape
    rows = S // GRID_W
    assert rows >= NA_BAND_ROWS and rows % NA_MAX_ROWS == 0
    rb = min(NA_BLOCK_ROWS, rows)
    halo = NA_MAX_ROWS * GRID_W
    per = rb // NA_MAX_ROWS
    last = rows // NA_MAX_ROWS - 1
    prev_idx = lambda i: jnp.maximum(i * per - 1, 0)
    next_idx = lambda i: jnp.minimum((i + 1) * per, last)
    k_main = pl.BlockSpec((None, rb * GRID_W, W), lambda b, i: (b, i, 0))
    k_prev = pl.BlockSpec((None, halo, W), lambda b, i: (b, prev_idx(i), 0))
    k_next = pl.BlockSpec((None, halo, W), lambda b, i: (b, next_idx(i), 0))
    t_main = pl.BlockSpec((None, W, rb * GRID_W), lambda b, i: (b, 0, i))
    t_prev = pl.BlockSpec((None, W, halo), lambda b, i: (b, 0, prev_idx(i)))
    t_next = pl.BlockSpec((None, W, halo), lambda b, i: (b, 0, next_idx(i)))
    kern = functools.partial(_na_kernel, rows=rows, rb=rb)
    band_tokens = (rb + 2 * NA_MAX_ROWS) * GRID_W
    return pl.pallas_call(
        kern,
        grid=(B, rows // rb),
        in_specs=[t_main, k_prev, k_main, k_next, t_prev, t_main, t_next, _const_spec(bias.shape)],
        out_specs=k_main,
        out_shape=jax.ShapeDtypeStruct((B, S, W), BF16),
        scratch_shapes=[pltpu.VMEM((band_tokens, W), BF16), pltpu.VMEM((W, band_tokens), BF16),
                        pltpu.VMEM((2, NA_BAND_ROWS * GRID_W, NA_GROUP_ROWS * GRID_W), F32)],
        compiler_params=_params(2, 56),
        name="natten",
    )(qbt, kb, kb, kb, vbt, vbt, vbt, bias)


def _merge_kernel(x_ref, oa_ref, ob_ref, nw_ref, wg_ref, bg_ref, wpa_ref, wpb_ref, wout_ref, pmn_ref, o_ref):
    x = x_ref[...]
    h = _rms(x, nw_ref[...]).astype(BF16)
    g = jnp.dot(h, wg_ref[...], preferred_element_type=F32) + bg_ref[...]
    gate = jax.nn.sigmoid(g)
    pa = jnp.dot(oa_ref[...], wpa_ref[...], preferred_element_type=F32)
    pb = jnp.dot(ob_ref[...], wpb_ref[...], preferred_element_type=F32)
    mix = gate[:, :D_MODEL] * pa + gate[:, D_MODEL:] * pb
    mo = jnp.dot(mix.astype(BF16), wout_ref[...], preferred_element_type=F32)
    o_ref[...] = x + _rms(mo, pmn_ref[...])


def _merge(x2, oa2, ob2, nw, wg, bg, wpa, wpb, wout, pmn):
    n, D = x2.shape
    tm = min(PROJ_TILE, n)
    row = lambda w: pl.BlockSpec((tm, w), lambda i: (i, 0))
    return pl.pallas_call(
        _merge_kernel,
        grid=(n // tm,),
        in_specs=[row(D), row(WIDTH_A), row(WIDTH_B), _const_spec((1, D)), _const_spec(wg.shape),
                  _const_spec((1, 2 * D)), _const_spec(wpa.shape), _const_spec(wpb.shape),
                  _const_spec(wout.shape), _const_spec((1, D))],
        out_specs=row(D),
        out_shape=jax.ShapeDtypeStruct((n, D), F32),
        compiler_params=_params(1, 40),
        name="merge",
    )(x2, oa2, ob2, nw, wg, bg, wpa, wpb, wout, pmn)


def _ffn_kernel(xm_ref, xp_ref, xn_ref, nw_ref, wup_ref, cw_ref, cb_ref, wdn_ref, pfn_ref, o_ref,
                u_scr, f_scr, *, tiles_per_seq):
    tm = xm_ref.shape[0]
    j = pl.program_id(0) % tiles_per_seq
    nw = nw_ref[...]
    xm = xm_ref[...]
    hp = _rms(xp_ref[...], nw) * (j > 0).astype(F32)
    hn = _rms(xn_ref[...], nw) * (j < tiles_per_seq - 1).astype(F32)
    h_ext = jnp.concatenate([hp, _rms(xm, nw), hn], axis=0).astype(BF16)
    for c in range(D_FF // FFN_CHUNK):
        conv = []
        for part in range(2):
            lo = part * D_FF + c * FFN_CHUNK
            cols = slice(lo, lo + FFN_CHUNK)
            u_scr[part] = jnp.dot(h_ext, wup_ref[:, cols], preferred_element_type=F32)
            acc = u_scr[part, pl.ds(HALO - 1, tm), :] * cw_ref[0:1, cols] + cb_ref[:, cols]
            acc = acc + u_scr[part, pl.ds(HALO, tm), :] * cw_ref[1:2, cols]
            acc = acc + u_scr[part, pl.ds(HALO + 1, tm), :] * cw_ref[2:3, cols]
            conv.append(acc)
        f = jax.nn.gelu(conv[0], approximate=True) * conv[1]
        f_scr[:, c * FFN_CHUNK:(c + 1) * FFN_CHUNK] = f.astype(BF16)
    y = jnp.dot(f_scr[...], wdn_ref[...], preferred_element_type=F32)
    o_ref[...] = xm + _rms(y, pfn_ref[...])


def _ffn(x2, seq_len, nw, wup, cw, cb, wdn, pfn):
    n, D = x2.shape
    tm = min(PROJ_TILE, seq_len)
    per = tm // HALO
    last = n // HALO - 1
    kern = functools.partial(_ffn_kernel, tiles_per_seq=seq_len // tm)
    return pl.pallas_call(
        kern,
        grid=(n // tm,),
        in_specs=[pl.BlockSpec((tm, D), lambda i: (i, 0)),
                  pl.BlockSpec((HALO, D), lambda i: (jnp.maximum(i * per - 1, 0), 0)),
                  pl.BlockSpec((HALO, D), lambda i: (jnp.minimum((i + 1) * per, last), 0)),
                  _const_spec((1, D)), _const_spec(wup.shape), _const_spec(cw.shape), _const_spec(cb.shape),
                  _const_spec(wdn.shape), _const_spec((1, D))],
        out_specs=pl.BlockSpec((tm, D), lambda i: (i, 0)),
        out_shape=jax.ShapeDtypeStruct((n, D), F32),
        scratch_shapes=[pltpu.VMEM((2, tm + 2 * HALO, FFN_CHUNK), F32), pltpu.VMEM((tm, D_FF), BF16)],
        compiler_params=_params(1, 48),
        name="ffn",
    )(x2, x2, x2, nw, wup, cw, cb, wdn, pfn)


def _rope_tables_t(seq_len):
    t = np.arange(seq_len)
    row = (t // GRID_W).astype(np.float32)
    col = (t % GRID_W).astype(np.float32)
    half = HEAD_DIM // 2
    freqs = (ROPE_THETA ** (-np.arange(0, half, 2, dtype=np.float32) / half)).astype(np.float32)
    ang_r = row[:, None] * freqs[None, :]
    ang_c = col[:, None] * freqs[None, :]
    ang = np.concatenate([ang_r, ang_r, ang_c, ang_c], axis=-1).astype(np.float32)
    return jnp.asarray(np.cos(ang).T, F32), jnp.asarray(np.sin(ang).T, F32)


def _prepare_weights(pre_mix_norm, w_in, b_gate, q_norm, k_norm, rpb, w_proj_a, w_proj_b, w_out,
                     post_mix_norm, pre_ffn_norm, w_up, conv_w, conv_b, w_down, post_ffn_norm):
    row = lambda v: v.reshape(1, -1).astype(F32)
    kb_lo = WIDTH_QKV_A + WIDTH_B
    kb_hi = kb_lo + WIDTH_B
    gate_lo = WIDTH_QKV_A + WIDTH_QKV_B
    return dict(
        nw1=row(pre_mix_norm),
        wt=jnp.concatenate([w_in[:, :kb_lo], w_in[:, kb_hi:gate_lo]], axis=1).T.astype(BF16),
        wkb=w_in[:, kb_lo:kb_hi].astype(BF16),
        wg=w_in[:, gate_lo:].astype(BF16),
        bg=row(b_gate),
        qn=q_norm.reshape(HEAD_DIM, 1).astype(F32),
        kn=k_norm.reshape(HEAD_DIM, 1).astype(F32),
        bias=_na_bias_table(rpb),
        wpa=w_proj_a.astype(BF16), wpb=w_proj_b.astype(BF16), wout=w_out.astype(BF16),
        pmn=row(post_mix_norm), nw2=row(pre_ffn_norm),
        wup=w_up.astype(BF16), cw=conv_w.astype(F32), cb=row(conv_b),
        wdn=w_down.astype(BF16), pfn=row(post_ffn_norm),
    )


def _encoder_layer(x, w):
    B, S, D = x.shape
    cos_t, sin_t = _rope_tables_t(S)
    qt, k, vt, qbt, kb, vbt = _project(x, w["nw1"], w["wt"], w["wkb"], cos_t, sin_t, w["qn"], w["kn"])
    oa = _gqa_attention(qt, k, vt)
    ob = _na_attention(qbt, kb, vbt, w["bias"])
    x2 = x.reshape(B * S, D)
    x2 = _merge(x2, oa.reshape(B * S, WIDTH_A), ob.reshape(B * S, WIDTH_B), w["nw1"], w["wg"], w["bg"],
                w["wpa"], w["wpb"], w["wout"], w["pmn"])
    x2 = _ffn(x2, S, w["nw2"], w["wup"], w["cw"], w["cb"], w["wdn"], w["pfn"])
    return x2.reshape(B, S, D)


def kernel(x_prompt, x_sample, pre_mix_norm, w_in, b_gate, q_norm, k_norm, rpb, w_proj_a, w_proj_b, w_out,
           post_mix_norm, pre_ffn_norm, w_up, conv_w, conv_b, w_down, post_ffn_norm):
    layers = [_prepare_weights(pre_mix_norm[l], w_in[l], b_gate[l], q_norm[l], k_norm[l], rpb[l],
                               w_proj_a[l], w_proj_b[l], w_out[l], post_mix_norm[l], pre_ffn_norm[l],
                               w_up[l], conv_w[l], conv_b[l], w_down[l], post_ffn_norm[l])
              for l in range(w_in.shape[0])]

    def run_trunk(x):
        for w in layers:
            x = _encoder_layer(x, w)
        return x

    return run_trunk(x_prompt), run_trunk(x_sample)
```

```python
import functools

import numpy as np
import jax
import jax.numpy as jnp
from jax import lax
from jax.experimental import pallas as pl
from jax.experimental.pallas import tpu as pltpu

D_MODEL = 1024
GRID_W = 64
HEAD_DIM = 64
N_HEADS_A = 8
N_KV_HEADS_A = 2
N_HEADS_B = 8
WIDTH_A = N_HEADS_A * HEAD_DIM
WIDTH_KV_A = N_KV_HEADS_A * HEAD_DIM
WIDTH_B = N_HEADS_B * HEAD_DIM
NA_MAX_ROWS = 8
NA_COLS = 16
ROPE_THETA = 10000.0
D_FF = 2816
CONV_WIDTH = 3
EPS = 1e-6
SCALE = HEAD_DIM ** -0.5
LOG2E = float(np.log2(np.e))
Q_SCALE = SCALE * LOG2E
GROUP_A = N_HEADS_A // N_KV_HEADS_A
WIDTH_QKV_A = WIDTH_A + 2 * WIDTH_KV_A
WIDTH_QKV_B = 3 * WIDTH_B

F32 = jnp.float32
BF16 = jnp.bfloat16
MIB = 1024 * 1024

SUBLANES = 8
LANES = 128
MXU_COUNT = 2
MXU_TILE = 256

PROJ_TILE = 512
GQA_TQ = 256
GQA_TK = 512
NA_BLOCK_ROWS = 32
NA_GROUP_ROWS = 4
NA_BAND_ROWS = 12
FFN_CHUNK = 256
HALO = SUBLANES


def _const_spec(shape):
    n = len(shape)
    return pl.BlockSpec(shape, lambda *_: (0,) * n, pipeline_mode=pl.Buffered(1))


def _params(n_axes, vmem_mib):
    return pltpu.CompilerParams(dimension_semantics=("parallel",) * n_axes,
                                vmem_limit_bytes=vmem_mib * MIB)


def _rms(x, w):
    return x * lax.rsqrt(jnp.mean(x * x, axis=-1, keepdims=True) + EPS) * w


def _proj_kernel(x_ref, nw_ref, wt_ref, wkb_ref, cos_ref, sin_ref, qn_ref, kn_ref,
                 qt_ref, k_ref, vt_ref, qbt_ref, kb_ref, vbt_ref):
    h = _rms(x_ref[...], nw_ref[...]).astype(BF16)
    pt = lax.dot_general(wt_ref[...], h, (((1,), (1,)), ((), ())), preferred_element_type=F32)
    cos = cos_ref[...]
    sin = sin_ref[...]

    def norm_rope(t, w):
        t = t * lax.rsqrt(jnp.mean(t * t, axis=0, keepdims=True) + EPS) * w
        q4 = HEAD_DIM // 4
        rot = jnp.concatenate([-t[q4:2 * q4], t[0:q4], -t[3 * q4:4 * q4], t[2 * q4:3 * q4]], axis=0)
        return t * cos + rot * sin

    qn = qn_ref[...]
    kn = kn_ref[...]
    for hh in range(N_HEADS_A):
        lo = hh * HEAD_DIM
        qt_ref[lo:lo + HEAD_DIM, :] = (norm_rope(pt[lo:lo + HEAD_DIM], qn) * Q_SCALE).astype(BF16)
    kt = jnp.concatenate(
        [norm_rope(pt[WIDTH_A + g * HEAD_DIM:WIDTH_A + (g + 1) * HEAD_DIM], kn) for g in range(N_KV_HEADS_A)],
        axis=0)
    k_ref[...] = kt.T.astype(BF16)
    vt_ref[...] = pt[WIDTH_A + WIDTH_KV_A:WIDTH_QKV_A].astype(BF16)
    qbt_ref[...] = (pt[WIDTH_QKV_A:WIDTH_QKV_A + WIDTH_B] * Q_SCALE).astype(BF16)
    vbt_ref[...] = pt[WIDTH_QKV_A + WIDTH_B:].astype(BF16)
    kb_ref[...] = jnp.dot(h, wkb_ref[...], preferred_element_type=F32).astype(BF16)


def _project(x, nw, wt, wkb, cos_t, sin_t, qn, kn):
    B, S, D = x.shape
    tm = min(PROJ_TILE, S)
    grid = (B, S // tm)
    row_blk = lambda w: pl.BlockSpec((None, tm, w), lambda b, i: (b, i, 0))
    col_blk = lambda w: pl.BlockSpec((None, w, tm), lambda b, i: (b, 0, i))
    return pl.pallas_call(
        _proj_kernel,
        grid=grid,
        in_specs=[row_blk(D), _const_spec((1, D)), _const_spec(wt.shape), _const_spec(wkb.shape),
                  pl.BlockSpec((HEAD_DIM, tm), lambda b, i: (0, i)),
                  pl.BlockSpec((HEAD_DIM, tm), lambda b, i: (0, i)),
                  _const_spec((HEAD_DIM, 1)), _const_spec((HEAD_DIM, 1))],
        out_specs=[col_blk(WIDTH_A), row_blk(WIDTH_KV_A), col_blk(WIDTH_KV_A),
                   col_blk(WIDTH_B), row_blk(WIDTH_B), col_blk(WIDTH_B)],
        out_shape=[jax.ShapeDtypeStruct((B, WIDTH_A, S), BF16),
                   jax.ShapeDtypeStruct((B, S, WIDTH_KV_A), BF16),
                   jax.ShapeDtypeStruct((B, WIDTH_KV_A, S), BF16),
                   jax.ShapeDtypeStruct((B, WIDTH_B, S), BF16),
                   jax.ShapeDtypeStruct((B, S, WIDTH_B), BF16),
                   jax.ShapeDtypeStruct((B, WIDTH_B, S), BF16)],
        compiler_params=_params(2, 40),
        name="proj",
    )(x, nw, wt, wkb, cos_t, sin_t, qn, kn)


def _gqa_kernel(qt_ref, k_ref, vt_ref, o_ref, qp_ref, m_ref, l_ref, acc_ref, s_scr, *, tk, n_kv):
    tq = qt_ref.shape[1]
    zeros = jnp.zeros((HEAD_DIM, tq), BF16)
    for hh in range(N_HEADS_A):
        g = hh // GROUP_A
        parts = [zeros] * N_KV_HEADS_A
        parts[g] = qt_ref[hh * HEAD_DIM:(hh + 1) * HEAD_DIM, :]
        qp_ref[hh] = jnp.concatenate(parts, axis=0)
    m_ref[...] = jnp.full(m_ref.shape, -jnp.inf, F32)
    l_ref[...] = jnp.zeros(l_ref.shape, F32)
    acc_ref[...] = jnp.zeros(acc_ref.shape, F32)

    def put_scores(slot, j, hh):
        half = tk // MXU_COUNT
        for part in range(MXU_COUNT):
            rows = pl.ds(pl.multiple_of(j * tk + part * half, half), half)
            s_scr[slot, part * half:(part + 1) * half, :] = jnp.dot(
                k_ref[rows, :], qp_ref[hh], preferred_element_type=F32)

    put_scores(0, 0, 0)

    def body(j, carry):
        start = pl.multiple_of(j * tk, tk)
        j_next = jnp.minimum(j + 1, n_kv - 1)
        for hh in range(N_HEADS_A):
            g = hh // GROUP_A
            nxt = (j, hh + 1) if hh + 1 < N_HEADS_A else (j_next, 0)
            put_scores((hh + 1) % 2, *nxt)
            s = s_scr[hh % 2]
            m_old = m_ref[hh]
            m_new = jnp.maximum(m_old, jnp.max(s, axis=0, keepdims=True))
            alpha = jnp.exp2(m_old - m_new)
            p = jnp.exp2(s - m_new)
            l_ref[hh] = alpha * l_ref[hh] + jnp.sum(p, axis=0, keepdims=True)
            m_ref[hh] = m_new
            pb = p.astype(BF16)
            pv = None
            for c in range(tk // MXU_TILE):
                vc = vt_ref[g * HEAD_DIM:(g + 1) * HEAD_DIM, pl.ds(start + c * MXU_TILE, MXU_TILE)]
                d = jnp.dot(vc, pb[c * MXU_TILE:(c + 1) * MXU_TILE], preferred_element_type=F32)
                pv = d if pv is None else pv + d
            acc_ref[hh] = alpha * acc_ref[hh] + pv
        return carry

    lax.fori_loop(0, n_kv, body, 0)
    for pair in range(N_HEADS_A // 2):
        o_pair = jnp.concatenate([acc_ref[hh] / l_ref[hh] for hh in (2 * pair, 2 * pair + 1)], axis=0)
        o_ref[:, pair * LANES:(pair + 1) * LANES] = o_pair.T.astype(BF16)


def _gqa_attention(qt, k, vt):
    B, _, S = qt.shape
    tq = min(GQA_TQ, S)
    tk = min(GQA_TK, S)
    kern = functools.partial(_gqa_kernel, tk=tk, n_kv=S // tk)
    return pl.pallas_call(
        kern,
        grid=(B, S // tq),
        in_specs=[pl.BlockSpec((None, WIDTH_A, tq), lambda b, i: (b, 0, i)),
                  pl.BlockSpec((None, S, WIDTH_KV_A), lambda b, i: (b, 0, 0)),
                  pl.BlockSpec((None, WIDTH_KV_A, S), lambda b, i: (b, 0, 0))],
        out_specs=pl.BlockSpec((None, tq, WIDTH_A), lambda b, i: (b, i, 0)),
        out_shape=jax.ShapeDtypeStruct((B, S, WIDTH_A), BF16),
        scratch_shapes=[pltpu.VMEM((N_HEADS_A, WIDTH_KV_A, tq), BF16),
                        pltpu.VMEM((N_HEADS_A, 1, tq), F32),
                        pltpu.VMEM((N_HEADS_A, 1, tq), F32),
                        pltpu.VMEM((N_HEADS_A, HEAD_DIM, tq), F32),
                        pltpu.VMEM((2, tk, tq), F32)],
        compiler_params=_params(2, 40),
        name="gqa",
    )(qt, k, vt)


def _na_kernel(qt_ref, kp_ref, km_ref, kn_ref, vp_ref, vm_ref, vn_ref, bias_ref, o_ref,
               kband, vband, s_scr, *, rows, rb):
    i = pl.program_id(1)
    halo = NA_MAX_ROWS * GRID_W
    main = rb * GRID_W
    kband[0:halo, :] = kp_ref[...]
    kband[halo:halo + main, :] = km_ref[...]
    kband[halo + main:, :] = kn_ref[...]
    vband[:, 0:halo] = vp_ref[...]
    vband[:, halo:halo + main] = vm_ref[...]
    vband[:, halo + main:] = vn_ref[...]
    nq = NA_GROUP_ROWS * GRID_W
    nk = NA_BAND_ROWS * GRID_W
    zeros = jnp.zeros((HEAD_DIM, nq), BF16)
    n_groups = rows // NA_GROUP_ROWS

    n_local = rb // NA_GROUP_ROWS

    def geometry(gl):
        grp = i * n_local + gl
        band_start = jnp.clip(grp * NA_GROUP_ROWS - NA_MAX_ROWS // 2, 0, rows - NA_BAND_ROWS)
        base = pl.multiple_of((band_start - i * rb + NA_MAX_ROWS) * GRID_W, nq)
        qcol = pl.multiple_of(gl * nq, nq)
        pat = jnp.where(grp == 0, 0, jnp.where(grp == n_groups - 1, 2, 1))
        return base, qcol, pat

    def put_scores(slot, geo, hh):
        base, qcol, _ = geo
        pair, sub = divmod(hh, 2)
        qh = qt_ref[hh * HEAD_DIM:(hh + 1) * HEAD_DIM, pl.ds(qcol, nq)]
        qp = jnp.concatenate([qh, zeros] if sub == 0 else [zeros, qh], axis=0)
        half = nk // MXU_COUNT
        for part in range(MXU_COUNT):
            kb = kband[pl.ds(base + part * half, half), pair * LANES:(pair + 1) * LANES]
            s_scr[slot, part * half:(part + 1) * half, :] = jnp.dot(kb, qp, preferred_element_type=F32)

    put_scores(0, geometry(0), 0)

    def group_body(gl, carry):
        geo = geometry(gl)
        geo_next = geometry(jnp.minimum(gl + 1, n_local - 1))
        base, qcol, pat = geo
        outs = []
        for hh in range(N_HEADS_B):
            nxt = (geo, hh + 1) if hh + 1 < N_HEADS_B else (geo_next, 0)
            put_scores((hh + 1) % 2, *nxt)
            s = s_scr[hh % 2] + bias_ref[pat * N_HEADS_B + hh]
            m = jnp.max(s, axis=0, keepdims=True)
            e = jnp.exp2(s - m)
            l = jnp.sum(e, axis=0, keepdims=True)
            eb = e.astype(BF16)
            pv = None
            for c in range(nk // MXU_TILE):
                vt = vband[hh * HEAD_DIM:(hh + 1) * HEAD_DIM, pl.ds(base + c * MXU_TILE, MXU_TILE)]
                d = jnp.dot(vt, eb[c * MXU_TILE:(c + 1) * MXU_TILE], preferred_element_type=F32)
                pv = d if pv is None else pv + d
            outs.append(pv / l)
            if hh % 2 == 1:
                o_pair = jnp.concatenate(outs, axis=0)
                o_ref[pl.ds(qcol, nq), (hh // 2) * LANES:(hh // 2 + 1) * LANES] = o_pair.T.astype(BF16)
                outs = []
        return carry

    lax.fori_loop(0, n_local, group_body, 0)


def _na_bias_table(rpb):
    c = np.arange(GRID_W)
    col_start = np.clip(c - NA_COLS // 2, 0, GRID_W - NA_COLS)
    in_win = (c[None, :] >= col_start[:, None]) & (c[None, :] < col_start[:, None] + NA_COLS)
    col_idx = np.clip(c[None, :] - c[:, None], -(NA_COLS - 1), NA_COLS - 1) + NA_COLS - 1
    i = np.arange(NA_GROUP_ROWS)
    k = np.arange(NA_BAND_ROWS)
    q_pos = np.stack([i, NA_MAX_ROWS // 2 + i, NA_MAX_ROWS + i])
    w_start = np.stack([0 * i, i, NA_MAX_ROWS // 2 + 0 * i])
    valid = (k[None, None, :] >= w_start[:, :, None]) & (k[None, None, :] < w_start[:, :, None] + NA_MAX_ROWS)
    rel = np.clip(k[None, None, :] - q_pos[:, :, None] + NA_MAX_ROWS - 1, 0, 2 * NA_MAX_ROWS - 2)
    vals = rpb[:, rel][..., col_idx]
    ok = jnp.asarray(valid[None, :, :, :, None, None] & in_win[None, None, None, None])
    tab = jnp.where(ok, vals, -jnp.inf)
    tab = tab.transpose(1, 0, 3, 5, 2, 4)
    tab = tab.reshape(3 * N_HEADS_B, NA_BAND_ROWS * GRID_W, NA_GROUP_ROWS * GRID_W).astype(F32)
    return tab * LOG2E


def _na_attention(qbt, kb, vbt, bias):
    B, S, W = kb.shape
    rows = S // GRID_W
    assert rows >= NA_BAND_ROWS and rows % NA_MAX_ROWS == 0
    rb = min(NA_BLOCK_ROWS, rows)
    halo = NA_MAX_ROWS * GRID_W
    per = rb // NA_MAX_ROWS
    last = rows // NA_MAX_ROWS - 1
    prev_idx = lambda i: jnp.maximum(i * per - 1, 0)
    next_idx = lambda i: jnp.minimum((i + 1) * per, last)
    k_main = pl.BlockSpec((None, rb * GRID_W, W), lambda b, i: (b, i, 0))
    k_prev = pl.BlockSpec((None, halo, W), lambda b, i: (b, prev_idx(i), 0))
    k_next = pl.BlockSpec((None, halo, W), lambda b, i: (b, next_idx(i), 0))
    t_main = pl.BlockSpec((None, W, rb * GRID_W), lambda b, i: (b, 0, i))
    t_prev = pl.BlockSpec((None, W, halo), lambda b, i: (b, 0, prev_idx(i)))
    t_next = pl.BlockSpec((None, W, halo), lambda b, i: (b, 0, next_idx(i)))
    kern = functools.partial(_na_kernel, rows=rows, rb=rb)
    band_tokens = (rb + 2 * NA_MAX_ROWS) * GRID_W
    return pl.pallas_call(
        kern,
        grid=(B, rows // rb),
        in_specs=[t_main, k_prev, k_main, k_next, t_prev, t_main, t_next, _const_spec(bias.shape)],
        out_specs=k_main,
        out_shape=jax.ShapeDtypeStruct((B, S, W), BF16),
        scratch_shapes=[pltpu.VMEM((band_tokens, W), BF16), pltpu.VMEM((W, band_tokens), BF16),
                        pltpu.VMEM((2, NA_BAND_ROWS * GRID_W, NA_GROUP_ROWS * GRID_W), F32)],
        compiler_params=_params(2, 56),
        name="natten",
    )(qbt, kb, kb, kb, vbt, vbt, vbt, bias)


def _merge_kernel(x_ref, oa_ref, ob_ref, nw_ref, wg_ref, bg_ref, wpa_ref, wpb_ref, wout_ref, pmn_ref, o_ref):
    x = x_ref[...]
    h = _rms(x, nw_ref[...]).astype(BF16)
    g = jnp.dot(h, wg_ref[...], preferred_element_type=F32) + bg_ref[...]
    gate = jax.nn.sigmoid(g)
    pa = jnp.dot(oa_ref[...], wpa_ref[...], preferred_element_type=F32)
    pb = jnp.dot(ob_ref[...], wpb_ref[...], preferred_element_type=F32)
    mix = gate[:, :D_MODEL] * pa + gate[:, D_MODEL:] * pb
    mo = jnp.dot(mix.astype(BF16), wout_ref[...], preferred_element_type=F32)
    o_ref[...] = x + _rms(mo, pmn_ref[...])


def _merge(x2, oa2, ob2, nw, wg, bg, wpa, wpb, wout, pmn):
    n, D = x2.shape
    tm = min(PROJ_TILE, n)
    row = lambda w: pl.BlockSpec((tm, w), lambda i: (i, 0))
    return pl.pallas_call(
        _merge_kernel,
        grid=(n // tm,),
        in_specs=[row(D), row(WIDTH_A), row(WIDTH_B), _const_spec((1, D)), _const_spec(wg.shape),
                  _const_spec((1, 2 * D)), _const_spec(wpa.shape), _const_spec(wpb.shape),
                  _const_spec(wout.shape), _const_spec((1, D))],
        out_specs=row(D),
        out_shape=jax.ShapeDtypeStruct((n, D), F32),
        compiler_params=_params(1, 40),
        name="merge",
    )(x2, oa2, ob2, nw, wg, bg, wpa, wpb, wout, pmn)


def _ffn_kernel(xm_ref, xp_ref, xn_ref, nw_ref, wup_ref, cw_ref, cb_ref, wdn_ref, pfn_ref, o_ref,
                u_scr, f_scr, *, tiles_per_seq):
    tm = xm_ref.shape[0]
    j = pl.program_id(0) % tiles_per_seq
    nw = nw_ref[...]
    xm = xm_ref[...]
    hp = _rms(xp_ref[...], nw) * (j > 0).astype(F32)
    hn = _rms(xn_ref[...], nw) * (j < tiles_per_seq - 1).astype(F32)
    h_ext = jnp.concatenate([hp, _rms(xm, nw), hn], axis=0).astype(BF16)
    for c in range(D_FF // FFN_CHUNK):
        conv = []
        for part in range(2):
            lo = part * D_FF + c * FFN_CHUNK
            cols = slice(lo, lo + FFN_CHUNK)
            u_scr[part] = jnp.dot(h_ext, wup_ref[:, cols], preferred_element_type=F32)
            acc = u_scr[part, pl.ds(HALO - 1, tm), :] * cw_ref[0:1, cols] + cb_ref[:, cols]
            acc = acc + u_scr[part, pl.ds(HALO, tm), :] * cw_ref[1:2, cols]
            acc = acc + u_scr[part, pl.ds(HALO + 1, tm), :] * cw_ref[2:3, cols]
            conv.append(acc)
        f = jax.nn.gelu(conv[0], approximate=True) * conv[1]
        f_scr[:, c * FFN_CHUNK:(c + 1) * FFN_CHUNK] = f.astype(BF16)
    y = jnp.dot(f_scr[...], wdn_ref[...], preferred_element_type=F32)
    o_ref[...] = xm + _rms(y, pfn_ref[...])


def _ffn(x2, seq_len, nw, wup, cw, cb, wdn, pfn):
    n, D = x2.shape
    tm = min(PROJ_TILE, seq_len)
    per = tm // HALO
    last = n // HALO - 1
    kern = functools.partial(_ffn_kernel, tiles_per_seq=seq_len // tm)
    return pl.pallas_call(
        kern,
        grid=(n // tm,),
        in_specs=[pl.BlockSpec((tm, D), lambda i: (i, 0)),
                  pl.BlockSpec((HALO, D), lambda i: (jnp.maximum(i * per - 1, 0), 0)),
                  pl.BlockSpec((HALO, D), lambda i: (jnp.minimum((i + 1) * per, last), 0)),
                  _const_spec((1, D)), _const_spec(wup.shape), _const_spec(cw.shape), _const_spec(cb.shape),
                  _const_spec(wdn.shape), _const_spec((1, D))],
        out_specs=pl.BlockSpec((tm, D), lambda i: (i, 0)),
        out_shape=jax.ShapeDtypeStruct((n, D), F32),
        scratch_shapes=[pltpu.VMEM((2, tm + 2 * HALO, FFN_CHUNK), F32), pltpu.VMEM((tm, D_FF), BF16)],
        compiler_params=_params(1, 48),
        name="ffn",
    )(x2, x2, x2, nw, wup, cw, cb, wdn, pfn)


def _rope_tables_t(seq_len):
    t = np.arange(seq_len)
    row = (t // GRID_W).astype(np.float32)
    col = (t % GRID_W).astype(np.float32)
    half = HEAD_DIM // 2
    freqs = (ROPE_THETA ** (-np.arange(0, half, 2, dtype=np.float32) / half)).astype(np.float32)
    ang_r = row[:, None] * freqs[None, :]
    ang_c = col[:, None] * freqs[None, :]
    ang = np.concatenate([ang_r, ang_r, ang_c, ang_c], axis=-1).astype(np.float32)
    return jnp.asarray(np.cos(ang).T, F32), jnp.asarray(np.sin(ang).T, F32)


def _prepare_weights(pre_mix_norm, w_in, b_gate, q_norm, k_norm, rpb, w_proj_a, w_proj_b, w_out,
                     post_mix_norm, pre_ffn_norm, w_up, conv_w, conv_b, w_down, post_ffn_norm):
    row = lambda v: v.reshape(1, -1).astype(F32)
    kb_lo = WIDTH_QKV_A + WIDTH_B
    kb_hi = kb_lo + WIDTH_B
    gate_lo = WIDTH_QKV_A + WIDTH_QKV_B
    return dict(
        nw1=row(pre_mix_norm),
        wt=jnp.concatenate([w_in[:, :kb_lo], w_in[:, kb_hi:gate_lo]], axis=1).T.astype(BF16),
        wkb=w_in[:, kb_lo:kb_hi].astype(BF16),
        wg=w_in[:, gate_lo:].astype(BF16),
        bg=row(b_gate),
        qn=q_norm.reshape(HEAD_DIM, 1).astype(F32),
        kn=k_norm.reshape(HEAD_DIM, 1).astype(F32),
        bias=_na_bias_table(rpb),
        wpa=w_proj_a.astype(BF16), wpb=w_proj_b.astype(BF16), wout=w_out.astype(BF16),
        pmn=row(post_mix_norm), nw2=row(pre_ffn_norm),
        wup=w_up.astype(BF16), cw=conv_w.astype(F32), cb=row(conv_b),
        wdn=w_down.astype(BF16), pfn=row(post_ffn_norm),
    )


def _encoder_layer(x, w):
    B, S, D = x.shape
    cos_t, sin_t = _rope_tables_t(S)
    qt, k, vt, qbt, kb, vbt = _project(x, w["nw1"], w["wt"], w["wkb"], cos_t, sin_t, w["qn"], w["kn"])
    oa = _gqa_attention(qt, k, vt)
    ob = _na_attention(qbt, kb, vbt, w["bias"])
    x2 = x.reshape(B * S, D)
    x2 = _merge(x2, oa.reshape(B * S, WIDTH_A), ob.reshape(B * S, WIDTH_B), w["nw1"], w["wg"], w["bg"],
                w["wpa"], w["wpb"], w["wout"], w["pmn"])
    x2 = _ffn(x2, S, w["nw2"], w["wup"], w["cw"], w["cb"], w["wdn"], w["pfn"])
    return x2.reshape(B, S, D)


def kernel(x_prompt, x_sample, pre_mix_norm, w_in, b_gate, q_norm, k_norm, rpb, w_proj_a, w_proj_b, w_out,
           post_mix_norm, pre_ffn_norm, w_up, conv_w, conv_b, w_down, post_ffn_norm):
    layers = [_prepare_weights(pre_mix_norm[l], w_in[l], b_gate[l], q_norm[l], k_norm[l], rpb[l],
                               w_proj_a[l], w_proj_b[l], w_out[l], post_mix_norm[l], pre_ffn_norm[l],
                               w_up[l], conv_w[l], conv_b[l], w_down[l], post_ffn_norm[l])
              for l in range(w_in.shape[0])]

    def run_trunk(x):
        for w in layers:
            x = _encoder_layer(x, w)
        return x

    return run_trunk(x_prompt), run_trunk(x_sample)
```

```python
import functools

import numpy as np
import jax
import jax.numpy as jnp
from jax import lax
from jax.experimental import pallas as pl
from jax.experimental.pallas import tpu as pltpu

D_MODEL = 1024
GRID_W = 64
HEAD_DIM = 64
N_HEADS_A = 8
N_KV_HEADS_A = 2
N_HEADS_B = 8
WIDTH_A = N_HEADS_A * HEAD_DIM
WIDTH_KV_A = N_KV_HEADS_A * HEAD_DIM
WIDTH_B = N_HEADS_B * HEAD_DIM
NA_MAX_ROWS = 8
NA_COLS = 16
ROPE_THETA = 10000.0
D_FF = 2816
CONV_WIDTH = 3
EPS = 1e-6
SCALE = HEAD_DIM ** -0.5
LOG2E = float(np.log2(np.e))
Q_SCALE = SCALE * LOG2E
GROUP_A = N_HEADS_A // N_KV_HEADS_A
WIDTH_QKV_A = WIDTH_A + 2 * WIDTH_KV_A
WIDTH_QKV_B = 3 * WIDTH_B

F32 = jnp.float32
BF16 = jnp.bfloat16
MIB = 1024 * 1024

SUBLANES = 8
LANES = 128
MXU_COUNT = 2
MXU_TILE = 256

ONES_ROWS = 16
V_ROWS = HEAD_DIM + ONES_ROWS

PROJ_TILE = 512
GQA_TQ = 256
GQA_TK = 512
NA_BLOCK_ROWS = 32
NA_GROUP_ROWS = 4
NA_BAND_ROWS = 12
SCORE_SLOTS = 4
SCORE_AHEAD = 3
FFN_CHUNK = 256
HALO = SUBLANES


def _const_spec(shape):
    n = len(shape)
    return pl.BlockSpec(shape, lambda *_: (0,) * n, pipeline_mode=pl.Buffered(1))


def _params(n_axes, vmem_mib):
    return pltpu.CompilerParams(dimension_semantics=("parallel",) * n_axes,
                                vmem_limit_bytes=vmem_mib * MIB)


def _rms(x, w):
    return x * lax.rsqrt(jnp.mean(x * x, axis=-1, keepdims=True) + EPS) * w


def _proj_kernel(x_ref, nw_ref, wt_ref, wkb_ref, cos_ref, sin_ref, qn_ref, kn_ref,
                 qt_ref, k_ref, vt_ref, qbt_ref, kb_ref, vbt_ref):
    h = _rms(x_ref[...], nw_ref[...]).astype(BF16)
    pt = lax.dot_general(wt_ref[...], h, (((1,), (1,)), ((), ())), preferred_element_type=F32)
    cos = cos_ref[...]
    sin = sin_ref[...]

    def norm_rope(t, w):
        t = t * lax.rsqrt(jnp.mean(t * t, axis=0, keepdims=True) + EPS) * w
        q4 = HEAD_DIM // 4
        rot = jnp.concatenate([-t[q4:2 * q4], t[0:q4], -t[3 * q4:4 * q4], t[2 * q4:3 * q4]], axis=0)
        return t * cos + rot * sin

    qn = qn_ref[...]
    kn = kn_ref[...]
    for hh in range(N_HEADS_A):
        lo = hh * HEAD_DIM
        qt_ref[lo:lo + HEAD_DIM, :] = (norm_rope(pt[lo:lo + HEAD_DIM], qn) * Q_SCALE).astype(BF16)
    kt = jnp.concatenate(
        [norm_rope(pt[WIDTH_A + g * HEAD_DIM:WIDTH_A + (g + 1) * HEAD_DIM], kn) for g in range(N_KV_HEADS_A)],
        axis=0)
    k_ref[...] = kt.T.astype(BF16)
    ones = jnp.ones((ONES_ROWS, pt.shape[1]), BF16)

    def put_values(ref, first_row, n_heads):
        for hh in range(n_heads):
            lo = first_row + hh * HEAD_DIM
            ref[hh * V_ROWS:hh * V_ROWS + HEAD_DIM, :] = pt[lo:lo + HEAD_DIM].astype(BF16)
            ref[hh * V_ROWS + HEAD_DIM:(hh + 1) * V_ROWS, :] = ones

    put_values(vt_ref, WIDTH_A + WIDTH_KV_A, N_KV_HEADS_A)
    qbt_ref[...] = (pt[WIDTH_QKV_A:WIDTH_QKV_A + WIDTH_B] * Q_SCALE).astype(BF16)
    put_values(vbt_ref, WIDTH_QKV_A + WIDTH_B, N_HEADS_B)
    kb_ref[...] = jnp.dot(h, wkb_ref[...], preferred_element_type=F32).astype(BF16)


def _project(x, nw, wt, wkb, cos_t, sin_t, qn, kn):
    B, S, D = x.shape
    tm = min(PROJ_TILE, S)
    grid = (B, S // tm)
    row_blk = lambda w: pl.BlockSpec((None, tm, w), lambda b, i: (b, i, 0))
    col_blk = lambda w: pl.BlockSpec((None, w, tm), lambda b, i: (b, 0, i))
    return pl.pallas_call(
        _proj_kernel,
        grid=grid,
        in_specs=[row_blk(D), _const_spec((1, D)), _const_spec(wt.shape), _const_spec(wkb.shape),
                  pl.BlockSpec((HEAD_DIM, tm), lambda b, i: (0, i)),
                  pl.BlockSpec((HEAD_DIM, tm), lambda b, i: (0, i)),
                  _const_spec((HEAD_DIM, 1)), _const_spec((HEAD_DIM, 1))],
        out_specs=[col_blk(WIDTH_A), row_blk(WIDTH_KV_A), col_blk(N_KV_HEADS_A * V_ROWS),
                   col_blk(WIDTH_B), row_blk(WIDTH_B), col_blk(N_HEADS_B * V_ROWS)],
        out_shape=[jax.ShapeDtypeStruct((B, WIDTH_A, S), BF16),
                   jax.ShapeDtypeStruct((B, S, WIDTH_KV_A), BF16),
                   jax.ShapeDtypeStruct((B, N_KV_HEADS_A * V_ROWS, S), BF16),
                   jax.ShapeDtypeStruct((B, WIDTH_B, S), BF16),
                   jax.ShapeDtypeStruct((B, S, WIDTH_B), BF16),
                   jax.ShapeDtypeStruct((B, N_HEADS_B * V_ROWS, S), BF16)],
        compiler_params=_params(2, 40),
        name="proj",
    )(x, nw, wt, wkb, cos_t, sin_t, qn, kn)


def _gqa_kernel(qt_ref, k_ref, vt_ref, o_ref, qp_ref, m_ref, acc_ref, s_scr, *, tk, n_kv):
    tq = qt_ref.shape[1]
    zeros = jnp.zeros((HEAD_DIM, tq), BF16)
    for hh in range(N_HEADS_A):
        g = hh // GROUP_A
        parts = [zeros] * N_KV_HEADS_A
        parts[g] = qt_ref[hh * HEAD_DIM:(hh + 1) * HEAD_DIM, :]
        qp_ref[hh] = jnp.concatenate(parts, axis=0)
    m_ref[...] = jnp.full(m_ref.shape, -jnp.inf, F32)
    acc_ref[...] = jnp.zeros(acc_ref.shape, F32)

    def put_scores(slot, j, hh):
        half = tk // MXU_COUNT
        for part in range(MXU_COUNT):
            rows = pl.ds(pl.multiple_of(j * tk + part * half, half), half)
            s_scr[slot, part * half:(part + 1) * half, :] = jnp.dot(
                k_ref[rows, :], qp_ref[hh], preferred_element_type=F32)

    for ahead in range(SCORE_AHEAD):
        put_scores(ahead, 0, ahead)

    def body(j, carry):
        start = pl.multiple_of(j * tk, tk)
        j_next = jnp.minimum(j + 1, n_kv - 1)
        for hh in range(N_HEADS_A):
            g = hh // GROUP_A
            ahead = hh + SCORE_AHEAD
            nxt = (j, ahead) if ahead < N_HEADS_A else (j_next, ahead - N_HEADS_A)
            put_scores(ahead % SCORE_SLOTS, *nxt)
            m_old = m_ref[hh]
            m_new = jnp.maximum(m_old, jnp.max(s_scr[hh % SCORE_SLOTS], axis=0, keepdims=True))
            alpha = jnp.exp2(m_old - m_new)
            pb = jnp.exp2(s_scr[hh % SCORE_SLOTS] - m_new).astype(BF16)
            m_ref[hh] = m_new
            pv = None
            for c in range(tk // MXU_TILE):
                vc = vt_ref[g * V_ROWS:(g + 1) * V_ROWS, pl.ds(start + c * MXU_TILE, MXU_TILE)]
                d = jnp.dot(vc, pb[c * MXU_TILE:(c + 1) * MXU_TILE], preferred_element_type=F32)
                pv = d if pv is None else pv + d
            acc_ref[hh] = alpha * acc_ref[hh] + pv
        return carry

    lax.fori_loop(0, n_kv, body, 0)
    for pair in range(N_HEADS_A // 2):
        o_pair = jnp.concatenate([acc_ref[hh, :HEAD_DIM] / acc_ref[hh, HEAD_DIM:HEAD_DIM + 1]
                                  for hh in (2 * pair, 2 * pair + 1)], axis=0)
        o_ref[:, pair * LANES:(pair + 1) * LANES] = o_pair.T.astype(BF16)


def _gqa_attention(qt, k, vt):
    B, _, S = qt.shape
    tq = min(GQA_TQ, S)
    tk = min(GQA_TK, S)
    kern = functools.partial(_gqa_kernel, tk=tk, n_kv=S // tk)
    return pl.pallas_call(
        kern,
        grid=(B, S // tq),
        in_specs=[pl.BlockSpec((None, WIDTH_A, tq), lambda b, i: (b, 0, i)),
                  pl.BlockSpec((None, S, WIDTH_KV_A), lambda b, i: (b, 0, 0)),
                  pl.BlockSpec((None, N_KV_HEADS_A * V_ROWS, S), lambda b, i: (b, 0, 0))],
        out_specs=pl.BlockSpec((None, tq, WIDTH_A), lambda b, i: (b, i, 0)),
        out_shape=jax.ShapeDtypeStruct((B, S, WIDTH_A), BF16),
        scratch_shapes=[pltpu.VMEM((N_HEADS_A, WIDTH_KV_A, tq), BF16),
                        pltpu.VMEM((N_HEADS_A, 1, tq), F32),
                        pltpu.VMEM((N_HEADS_A, V_ROWS, tq), F32),
                        pltpu.VMEM((SCORE_SLOTS, tk, tq), F32)],
        compiler_params=_params(2, 40),
        name="gqa",
    )(qt, k, vt)


def _na_kernel(qt_ref, kp_ref, km_ref, kn_ref, vp_ref, vm_ref, vn_ref, bias_ref, o_ref,
               kband, vband, s_scr, *, rows, rb):
    i = pl.program_id(1)
    halo = NA_MAX_ROWS * GRID_W
    main = rb * GRID_W
    kband[0:halo, :] = kp_ref[...]
    kband[halo:halo + main, :] = km_ref[...]
    kband[halo + main:, :] = kn_ref[...]
    vband[:, 0:halo] = vp_ref[...]
    vband[:, halo:halo + main] = vm_ref[...]
    vband[:, halo + main:] = vn_ref[...]
    nq = NA_GROUP_ROWS * GRID_W
    nk = NA_BAND_ROWS * GRID_W
    zeros = jnp.zeros((HEAD_DIM, nq), BF16)
    n_groups = rows // NA_GROUP_ROWS

    n_local = rb // NA_GROUP_ROWS

    def geometry(gl):
        grp = i * n_local + gl
        band_start = jnp.clip(grp * NA_GROUP_ROWS - NA_MAX_ROWS // 2, 0, rows - NA_BAND_ROWS)
        base = pl.multiple_of((band_start - i * rb + NA_MAX_ROWS) * GRID_W, nq)
        qcol = pl.multiple_of(gl * nq, nq)
        pat = jnp.where(grp == 0, 0, jnp.where(grp == n_groups - 1, 2, 1))
        return base, qcol, pat

    def put_scores(slot, geo, hh):
        base, qcol, _ = geo
        pair, sub = divmod(hh, 2)
        qh = qt_ref[hh * HEAD_DIM:(hh + 1) * HEAD_DIM, pl.ds(qcol, nq)]
        qp = jnp.concatenate([qh, zeros] if sub == 0 else [zeros, qh], axis=0)
        half = nk // MXU_COUNT
        for part in range(MXU_COUNT):
            kb = kband[pl.ds(base + part * half, half), pair * LANES:(pair + 1) * LANES]
            s_scr[slot, part * half:(part + 1) * half, :] = jnp.dot(kb, qp, preferred_element_type=F32)

    for ahead in range(SCORE_AHEAD):
        put_scores(ahead, geometry(0), ahead)

    def group_body(gl, carry):
        geo = geometry(gl)
        geo_next = geometry(jnp.minimum(gl + 1, n_local - 1))
        base, qcol, pat = geo
        outs = []
        for hh in range(N_HEADS_B):
            ahead = hh + SCORE_AHEAD
            nxt = (geo, ahead) if ahead < N_HEADS_B else (geo_next, ahead - N_HEADS_B)
            put_scores(ahead % SCORE_SLOTS, *nxt)
            s = s_scr[hh % SCORE_SLOTS] + bias_ref[hh * 3 + pat]
            m = jnp.max(s, axis=0, keepdims=True)
            eb = jnp.exp2(s - m).astype(BF16)
            pv = None
            for c in range(nk // MXU_TILE):
                vt = vband[hh * V_ROWS:(hh + 1) * V_ROWS, pl.ds(base + c * MXU_TILE, MXU_TILE)]
                d = jnp.dot(vt, eb[c * MXU_TILE:(c + 1) * MXU_TILE], preferred_element_type=F32)
                pv = d if pv is None else pv + d
            outs.append(pv[:HEAD_DIM] / pv[HEAD_DIM:HEAD_DIM + 1])
            if hh % 2 == 1:
                o_pair = jnp.concatenate(outs, axis=0)
                o_ref[pl.ds(qcol, nq), (hh // 2) * LANES:(hh // 2 + 1) * LANES] = o_pair.T.astype(BF16)
                outs = []
        return carry

    lax.fori_loop(0, n_local, group_body, 0)


def _na_bias_table(rpb):
    kk = np.arange(NA_BAND_ROWS * GRID_W)[:, None]
    qq = np.arange(NA_GROUP_ROWS * GRID_W)[None, :]
    kr, kc = kk // GRID_W, kk % GRID_W
    qi, qc = qq // GRID_W, qq % GRID_W
    col_start = np.clip(qc - NA_COLS // 2, 0, GRID_W - NA_COLS)
    in_win = (kc >= col_start) & (kc < col_start + NA_COLS)
    col_idx = np.clip(kc - qc, -(NA_COLS - 1), NA_COLS - 1) + NA_COLS - 1
    q_pos = np.stack([qi, NA_MAX_ROWS // 2 + qi, NA_MAX_ROWS + qi])
    w_start = np.stack([0 * qi, qi, NA_MAX_ROWS // 2 + 0 * qi])
    valid = (kr[None] >= w_start) & (kr[None] < w_start + NA_MAX_ROWS) & in_win[None]
    rel = np.clip(kr[None] - q_pos + NA_MAX_ROWS - 1, 0, 2 * NA_MAX_ROWS - 2)
    vals = rpb[:, rel, np.broadcast_to(col_idx, rel.shape)]
    tab = jnp.where(jnp.asarray(valid)[None], vals * LOG2E, -jnp.inf)
    return tab.reshape(N_HEADS_B * 3, NA_BAND_ROWS * GRID_W, NA_GROUP_ROWS * GRID_W).astype(F32)


def _na_attention(qbt, kb, vbt, bias):
    B, S, W = kb.shape
    rows = S // GRID_W
    assert rows >= NA_BAND_ROWS and rows % NA_MAX_ROWS == 0
    rb = min(NA_BLOCK_ROWS, rows)
    halo = NA_MAX_ROWS * GRID_W
    per = rb // NA_MAX_ROWS
    last = rows // NA_MAX_ROWS - 1
    prev_idx = lambda i: jnp.maximum(i * per - 1, 0)
    next_idx = lambda i: jnp.minimum((i + 1) * per, last)
    k_main = pl.BlockSpec((None, rb * GRID_W, W), lambda b, i: (b, i, 0))
    k_prev = pl.BlockSpec((None, halo, W), lambda b, i: (b, prev_idx(i), 0))
    k_next = pl.BlockSpec((None, halo, W), lambda b, i: (b, next_idx(i), 0))
    q_main = pl.BlockSpec((None, W, rb * GRID_W), lambda b, i: (b, 0, i))
    vw = N_HEADS_B * V_ROWS
    v_main = pl.BlockSpec((None, vw, rb * GRID_W), lambda b, i: (b, 0, i))
    v_prev = pl.BlockSpec((None, vw, halo), lambda b, i: (b, 0, prev_idx(i)))
    v_next = pl.BlockSpec((None, vw, halo), lambda b, i: (b, 0, next_idx(i)))
    kern = functools.partial(_na_kernel, rows=rows, rb=rb)
    band_tokens = (rb + 2 * NA_MAX_ROWS) * GRID_W
    return pl.pallas_call(
        kern,
        grid=(B, rows // rb),
        in_specs=[q_main, k_prev, k_main, k_next, v_prev, v_main, v_next, _const_spec(bias.shape)],
        out_specs=k_main,
        out_shape=jax.ShapeDtypeStruct((B, S, W), BF16),
        scratch_shapes=[pltpu.VMEM((band_tokens, W), BF16), pltpu.VMEM((vw, band_tokens), BF16),
                        pltpu.VMEM((SCORE_SLOTS, NA_BAND_ROWS * GRID_W, NA_GROUP_ROWS * GRID_W), F32)],
        compiler_params=_params(2, 56),
        name="natten",
    )(qbt, kb, kb, kb, vbt, vbt, vbt, bias)


def _merge_kernel(x_ref, oa_ref, ob_ref, nw_ref, wg_ref, bg_ref, wpa_ref, wpb_ref, wout_ref, pmn_ref, o_ref):
    x = x_ref[...]
    h = _rms(x, nw_ref[...]).astype(BF16)
    g = jnp.dot(h, wg_ref[...], preferred_element_type=F32) + bg_ref[...]
    gate = jax.nn.sigmoid(g)
    pa = jnp.dot(oa_ref[...], wpa_ref[...], preferred_element_type=F32)
    pb = jnp.dot(ob_ref[...], wpb_ref[...], preferred_element_type=F32)
    mix = gate[:, :D_MODEL] * pa + gate[:, D_MODEL:] * pb
    mo = jnp.dot(mix.astype(BF16), wout_ref[...], preferred_element_type=F32)
    o_ref[...] = x + _rms(mo, pmn_ref[...])


def _merge(x2, oa2, ob2, nw, wg, bg, wpa, wpb, wout, pmn):
    n, D = x2.shape
    tm = min(PROJ_TILE, n)
    row = lambda w: pl.BlockSpec((tm, w), lambda i: (i, 0))
    return pl.pallas_call(
        _merge_kernel,
        grid=(n // tm,),
        in_specs=[row(D), row(WIDTH_A), row(WIDTH_B), _const_spec((1, D)), _const_spec(wg.shape),
                  _const_spec((1, 2 * D)), _const_spec(wpa.shape), _const_spec(wpb.shape),
                  _const_spec(wout.shape), _const_spec((1, D))],
        out_specs=row(D),
        out_shape=jax.ShapeDtypeStruct((n, D), F32),
        compiler_params=_params(1, 40),
        name="merge",
    )(x2, oa2, ob2, nw, wg, bg, wpa, wpb, wout, pmn)


def _ffn_kernel(xm_ref, xp_ref, xn_ref, nw_ref, wup_ref, cw_ref, cb_ref, wdn_ref, pfn_ref, o_ref,
                u_scr, f_scr, *, tiles_per_seq):
    tm = xm_ref.shape[0]
    j = pl.program_id(0) % tiles_per_seq
    nw = nw_ref[...]
    xm = xm_ref[...]
    hp = _rms(xp_ref[...], nw) * (j > 0).astype(F32)
    hn = _rms(xn_ref[...], nw) * (j < tiles_per_seq - 1).astype(F32)
    h_ext = jnp.concatenate([hp, _rms(xm, nw), hn], axis=0).astype(BF16)
    for c in range(D_FF // FFN_CHUNK):
        conv = []
        for part in range(2):
            lo = part * D_FF + c * FFN_CHUNK
            cols = slice(lo, lo + FFN_CHUNK)
            u_scr[part] = jnp.dot(h_ext, wup_ref[:, cols], preferred_element_type=F32)
            acc = u_scr[part, pl.ds(HALO - 1, tm), :] * cw_ref[0:1, cols] + cb_ref[:, cols]
            acc = acc + u_scr[part, pl.ds(HALO, tm), :] * cw_ref[1:2, cols]
            acc = acc + u_scr[part, pl.ds(HALO + 1, tm), :] * cw_ref[2:3, cols]
            conv.append(acc)
        f = jax.nn.gelu(conv[0], approximate=True) * conv[1]
        f_scr[:, c * FFN_CHUNK:(c + 1) * FFN_CHUNK] = f.astype(BF16)
    y = jnp.dot(f_scr[...], wdn_ref[...], preferred_element_type=F32)
    o_ref[...] = xm + _rms(y, pfn_ref[...])


def _ffn(x2, seq_len, nw, wup, cw, cb, wdn, pfn):
    n, D = x2.shape
    tm = min(PROJ_TILE, seq_len)
    per = tm // HALO
    last = n // HALO - 1
    kern = functools.partial(_ffn_kernel, tiles_per_seq=seq_len // tm)
    return pl.pallas_call(
        kern,
        grid=(n // tm,),
        in_specs=[pl.BlockSpec((tm, D), lambda i: (i, 0)),
                  pl.BlockSpec((HALO, D), lambda i: (jnp.maximum(i * per - 1, 0), 0)),
                  pl.BlockSpec((HALO, D), lambda i: (jnp.minimum((i + 1) * per, last), 0)),
                  _const_spec((1, D)), _const_spec(wup.shape), _const_spec(cw.shape), _const_spec(cb.shape),
                  _const_spec(wdn.shape), _const_spec((1, D))],
        out_specs=pl.BlockSpec((tm, D), lambda i: (i, 0)),
        out_shape=jax.ShapeDtypeStruct((n, D), F32),
        scratch_shapes=[pltpu.VMEM((2, tm + 2 * HALO, FFN_CHUNK), F32), pltpu.VMEM((tm, D_FF), BF16)],
        compiler_params=_params(1, 48),
        name="ffn",
    )(x2, x2, x2, nw, wup, cw, cb, wdn, pfn)


def _rope_tables_t(seq_len):
    t = np.arange(seq_len)
    row = (t // GRID_W).astype(np.float32)
    col = (t % GRID_W).astype(np.float32)
    half = HEAD_DIM // 2
    freqs = (ROPE_THETA ** (-np.arange(0, half, 2, dtype=np.float32) / half)).astype(np.float32)
    ang_r = row[:, None] * freqs[None, :]
    ang_c = col[:, None] * freqs[None, :]
    ang = np.concatenate([ang_r, ang_r, ang_c, ang_c], axis=-1).astype(np.float32)
    return jnp.asarray(np.cos(ang).T, F32), jnp.asarray(np.sin(ang).T, F32)


def _prepare_weights(pre_mix_norm, w_in, b_gate, q_norm, k_norm, rpb, w_proj_a, w_proj_b, w_out,
                     post_mix_norm, pre_ffn_norm, w_up, conv_w, conv_b, w_down, post_ffn_norm):
    row = lambda v: v.reshape(1, -1).astype(F32)
    kb_lo = WIDTH_QKV_A + WIDTH_B
    kb_hi = kb_lo + WIDTH_B
    gate_lo = WIDTH_QKV_A + WIDTH_QKV_B
    return dict(
        nw1=row(pre_mix_norm),
        wt=jnp.concatenate([w_in[:, :kb_lo], w_in[:, kb_hi:gate_lo]], axis=1).T.astype(BF16),
        wkb=w_in[:, kb_lo:kb_hi].astype(BF16),
        wg=w_in[:, gate_lo:].astype(BF16),
        bg=row(b_gate),
        qn=q_norm.reshape(HEAD_DIM, 1).astype(F32),
        kn=k_norm.reshape(HEAD_DIM, 1).astype(F32),
        bias=_na_bias_table(rpb),
        wpa=w_proj_a.astype(BF16), wpb=w_proj_b.astype(BF16), wout=w_out.astype(BF16),
        pmn=row(post_mix_norm), nw2=row(pre_ffn_norm),
        wup=w_up.astype(BF16), cw=conv_w.astype(F32), cb=row(conv_b),
        wdn=w_down.astype(BF16), pfn=row(post_ffn_norm),
    )


def _encoder_layer(x, w):
    B, S, D = x.shape
    cos_t, sin_t = _rope_tables_t(S)
    qt, k, vt, qbt, kb, vbt = _project(x, w["nw1"], w["wt"], w["wkb"], cos_t, sin_t, w["qn"], w["kn"])
    oa = _gqa_attention(qt, k, vt)
    ob = _na_attention(qbt, kb, vbt, w["bias"])
    x2 = x.reshape(B * S, D)
    x2 = _merge(x2, oa.reshape(B * S, WIDTH_A), ob.reshape(B * S, WIDTH_B), w["nw1"], w["wg"], w["bg"],
                w["wpa"], w["wpb"], w["wout"], w["pmn"])
    x2 = _ffn(x2, S, w["nw2"], w["wup"], w["cw"], w["cb"], w["wdn"], w["pfn"])
    return x2.reshape(B, S, D)


def kernel(x_prompt, x_sample, pre_mix_norm, w_in, b_gate, q_norm, k_norm, rpb, w_proj_a, w_proj_b, w_out,
           post_mix_norm, pre_ffn_norm, w_up, conv_w, conv_b, w_down, post_ffn_norm):
    layers = [_prepare_weights(pre_mix_norm[l], w_in[l], b_gate[l], q_norm[l], k_norm[l], rpb[l],
                               w_proj_a[l], w_proj_b[l], w_out[l], post_mix_norm[l], pre_ffn_norm[l],
                               w_up[l], conv_w[l], conv_b[l], w_down[l], post_ffn_norm[l])
              for l in range(w_in.shape[0])]

    def run_trunk(x):
        for w in layers:
            x = _encoder_layer(x, w)
        return x

    return run_trunk(x_prompt), run_trunk(x_sample)
```

```python
import functools

import numpy as np
import jax
import jax.numpy as jnp
from jax import lax
from jax.experimental import pallas as pl
from jax.experimental.pallas import tpu as pltpu

D_MODEL = 1024
GRID_W = 64
HEAD_DIM = 64
N_HEADS_A = 8
N_KV_HEADS_A = 2
N_HEADS_B = 8
WIDTH_A = N_HEADS_A * HEAD_DIM
WIDTH_KV_A = N_KV_HEADS_A * HEAD_DIM
WIDTH_B = N_HEADS_B * HEAD_DIM
NA_MAX_ROWS = 8
NA_COLS = 16
ROPE_THETA = 10000.0
D_FF = 2816
CONV_WIDTH = 3
EPS = 1e-6
SCALE = HEAD_DIM ** -0.5
LOG2E = float(np.log2(np.e))
Q_SCALE = SCALE * LOG2E
GROUP_A = N_HEADS_A // N_KV_HEADS_A
WIDTH_QKV_A = WIDTH_A + 2 * WIDTH_KV_A
WIDTH_QKV_B = 3 * WIDTH_B

F32 = jnp.float32
BF16 = jnp.bfloat16
MIB = 1024 * 1024

SUBLANES = 8
LANES = 128
MXU_COUNT = 2
MXU_TILE = 256

ONES_ROWS = 16
V_ROWS = HEAD_DIM + ONES_ROWS

PROJ_TILE = 512
GQA_TQ = 256
GQA_TK = 512
NA_BLOCK_ROWS = 32
NA_GROUP_ROWS = 4
NA_BAND_ROWS = 12
SCORE_SLOTS = 4
SCORE_AHEAD = 3
FFN_CHUNK = 256
HALO = SUBLANES


def _const_spec(shape):
    n = len(shape)
    return pl.BlockSpec(shape, lambda *_: (0,) * n, pipeline_mode=pl.Buffered(1))


def _params(n_axes, vmem_mib):
    return pltpu.CompilerParams(dimension_semantics=("parallel",) * n_axes,
                                vmem_limit_bytes=vmem_mib * MIB)


def _rms(x, w):
    return x * lax.rsqrt(jnp.mean(x * x, axis=-1, keepdims=True) + EPS) * w


def _proj_kernel(x_ref, nw_ref, wt_ref, wkb_ref, cos_ref, sin_ref, qn_ref, kn_ref,
                 qt_ref, k_ref, vt_ref, qbt_ref, kb_ref, vbt_ref):
    h = _rms(x_ref[...], nw_ref[...]).astype(BF16)
    pt = lax.dot_general(wt_ref[...], h, (((1,), (1,)), ((), ())), preferred_element_type=F32)
    cos = cos_ref[...]
    sin = sin_ref[...]

    def norm_rope(t, w):
        t = t * lax.rsqrt(jnp.mean(t * t, axis=0, keepdims=True) + EPS) * w
        q4 = HEAD_DIM // 4
        rot = jnp.concatenate([-t[q4:2 * q4], t[0:q4], -t[3 * q4:4 * q4], t[2 * q4:3 * q4]], axis=0)
        return t * cos + rot * sin

    qn = qn_ref[...]
    kn = kn_ref[...]
    for hh in range(N_HEADS_A):
        lo = hh * HEAD_DIM
        qt_ref[lo:lo + HEAD_DIM, :] = (norm_rope(pt[lo:lo + HEAD_DIM], qn) * Q_SCALE).astype(BF16)
    kt = jnp.concatenate(
        [norm_rope(pt[WIDTH_A + g * HEAD_DIM:WIDTH_A + (g + 1) * HEAD_DIM], kn) for g in range(N_KV_HEADS_A)],
        axis=0)
    k_ref[...] = kt.T.astype(BF16)
    ones = jnp.ones((ONES_ROWS, pt.shape[1]), BF16)

    def put_values(ref, first_row, n_heads):
        for hh in range(n_heads):
            lo = first_row + hh * HEAD_DIM
            ref[hh * V_ROWS:hh * V_ROWS + HEAD_DIM, :] = pt[lo:lo + HEAD_DIM].astype(BF16)
            ref[hh * V_ROWS + HEAD_DIM:(hh + 1) * V_ROWS, :] = ones

    put_values(vt_ref, WIDTH_A + WIDTH_KV_A, N_KV_HEADS_A)
    qbt_ref[...] = (pt[WIDTH_QKV_A:WIDTH_QKV_A + WIDTH_B] * Q_SCALE).astype(BF16)
    put_values(vbt_ref, WIDTH_QKV_A + WIDTH_B, N_HEADS_B)
    kb_ref[...] = jnp.dot(h, wkb_ref[...], preferred_element_type=F32).astype(BF16)


def _project(x, nw, wt, wkb, cos_t, sin_t, qn, kn):
    B, S, D = x.shape
    tm = min(PROJ_TILE, S)
    grid = (B, S // tm)
    row_blk = lambda w: pl.BlockSpec((None, tm, w), lambda b, i: (b, i, 0))
    col_blk = lambda w: pl.BlockSpec((None, w, tm), lambda b, i: (b, 0, i))
    return pl.pallas_call(
        _proj_kernel,
        grid=grid,
        in_specs=[row_blk(D), _const_spec((1, D)), _const_spec(wt.shape), _const_spec(wkb.shape),
                  pl.BlockSpec((HEAD_DIM, tm), lambda b, i: (0, i)),
                  pl.BlockSpec((HEAD_DIM, tm), lambda b, i: (0, i)),
                  _const_spec((HEAD_DIM, 1)), _const_spec((HEAD_DIM, 1))],
        out_specs=[col_blk(WIDTH_A), row_blk(WIDTH_KV_A), col_blk(N_KV_HEADS_A * V_ROWS),
                   col_blk(WIDTH_B), row_blk(WIDTH_B), col_blk(N_HEADS_B * V_ROWS)],
        out_shape=[jax.ShapeDtypeStruct((B, WIDTH_A, S), BF16),
                   jax.ShapeDtypeStruct((B, S, WIDTH_KV_A), BF16),
                   jax.ShapeDtypeStruct((B, N_KV_HEADS_A * V_ROWS, S), BF16),
                   jax.ShapeDtypeStruct((B, WIDTH_B, S), BF16),
                   jax.ShapeDtypeStruct((B, S, WIDTH_B), BF16),
                   jax.ShapeDtypeStruct((B, N_HEADS_B * V_ROWS, S), BF16)],
        compiler_params=_params(2, 40),
        name="proj",
    )(x, nw, wt, wkb, cos_t, sin_t, qn, kn)


def _gqa_kernel(qt_ref, k_ref, vt_ref, o_ref, qp_ref, m_ref, acc_ref, s_scr, *, tk, n_kv):
    tq = qt_ref.shape[1]
    zeros = jnp.zeros((HEAD_DIM, tq), BF16)
    for hh in range(N_HEADS_A):
        g = hh // GROUP_A
        parts = [zeros] * N_KV_HEADS_A
        parts[g] = qt_ref[hh * HEAD_DIM:(hh + 1) * HEAD_DIM, :]
        qp_ref[hh] = jnp.concatenate(parts, axis=0)
    m_ref[...] = jnp.full(m_ref.shape, -jnp.inf, F32)
    acc_ref[...] = jnp.zeros(acc_ref.shape, F32)

    def put_scores(slot, j, hh):
        half = tk // MXU_COUNT
        for part in range(MXU_COUNT):
            rows = pl.ds(pl.multiple_of(j * tk + part * half, half), half)
            s_scr[slot, part * half:(part + 1) * half, :] = jnp.dot(
                k_ref[rows, :], qp_ref[hh], preferred_element_type=F32)

    for ahead in range(SCORE_AHEAD):
        put_scores(ahead, 0, ahead)

    def body(j, carry):
        start = pl.multiple_of(j * tk, tk)
        j_next = jnp.minimum(j + 1, n_kv - 1)
        for hh in range(N_HEADS_A):
            g = hh // GROUP_A
            ahead = hh + SCORE_AHEAD
            nxt = (j, ahead) if ahead < N_HEADS_A else (j_next, ahead - N_HEADS_A)
            put_scores(ahead % SCORE_SLOTS, *nxt)
            m_old = m_ref[hh]
            m_new = jnp.maximum(m_old, jnp.max(s_scr[hh % SCORE_SLOTS], axis=0, keepdims=True))
            alpha = jnp.exp2(m_old - m_new)
            pb = jnp.exp2(s_scr[hh % SCORE_SLOTS] - m_new).astype(BF16)
            m_ref[hh] = m_new
            pv = None
            for c in range(tk // MXU_TILE):
                vc = vt_ref[g * V_ROWS:(g + 1) * V_ROWS, pl.ds(start + c * MXU_TILE, MXU_TILE)]
                d = jnp.dot(vc, pb[c * MXU_TILE:(c + 1) * MXU_TILE], preferred_element_type=F32)
                pv = d if pv is None else pv + d
            acc_ref[hh] = alpha * acc_ref[hh] + pv
        return carry

    lax.fori_loop(0, n_kv, body, 0)
    for pair in range(N_HEADS_A // 2):
        o_pair = jnp.concatenate([acc_ref[hh, :HEAD_DIM] / acc_ref[hh, HEAD_DIM:HEAD_DIM + 1]
                                  for hh in (2 * pair, 2 * pair + 1)], axis=0)
        o_ref[:, pair * LANES:(pair + 1) * LANES] = o_pair.T.astype(BF16)


def _gqa_attention(qt, k, vt):
    B, _, S = qt.shape
    tq = min(GQA_TQ, S)
    tk = min(GQA_TK, S)
    kern = functools.partial(_gqa_kernel, tk=tk, n_kv=S // tk)
    return pl.pallas_call(
        kern,
        grid=(B, S // tq),
        in_specs=[pl.BlockSpec((None, WIDTH_A, tq), lambda b, i: (b, 0, i)),
                  pl.BlockSpec((None, S, WIDTH_KV_A), lambda b, i: (b, 0, 0)),
                  pl.BlockSpec((None, N_KV_HEADS_A * V_ROWS, S), lambda b, i: (b, 0, 0))],
        out_specs=pl.BlockSpec((None, tq, WIDTH_A), lambda b, i: (b, i, 0)),
        out_shape=jax.ShapeDtypeStruct((B, S, WIDTH_A), BF16),
        scratch_shapes=[pltpu.VMEM((N_HEADS_A, WIDTH_KV_A, tq), BF16),
                        pltpu.VMEM((N_HEADS_A, 1, tq), F32),
                        pltpu.VMEM((N_HEADS_A, V_ROWS, tq), F32),
                        pltpu.VMEM((SCORE_SLOTS, tk, tq), F32)],
        compiler_params=_params(2, 40),
        name="gqa",
    )(qt, k, vt)


def _na_kernel(qt_ref, kp_ref, km_ref, kn_ref, vp_ref, vm_ref, vn_ref, bias_ref, o_ref,
               kband, vband, s_scr, *, rows, rb):
    i = pl.program_id(1)
    halo = NA_MAX_ROWS * GRID_W
    main = rb * GRID_W
    kband[0:halo, :] = kp_ref[...]
    kband[halo:halo + main, :] = km_ref[...]
    kband[halo + main:, :] = kn_ref[...]
    vband[:, 0:halo] = vp_ref[...]
    vband[:, halo:halo + main] = vm_ref[...]
    vband[:, halo + main:] = vn_ref[...]
    nq = NA_GROUP_ROWS * GRID_W
    nk = NA_BAND_ROWS * GRID_W
    zeros = jnp.zeros((HEAD_DIM, nq), BF16)
    n_groups = rows // NA_GROUP_ROWS

    n_local = rb // NA_GROUP_ROWS

    def geometry(gl):
        grp = i * n_local + gl
        band_start = jnp.clip(grp * NA_GROUP_ROWS - NA_MAX_ROWS // 2, 0, rows - NA_BAND_ROWS)
        base = pl.multiple_of((band_start - i * rb + NA_MAX_ROWS) * GRID_W, nq)
        qcol = pl.multiple_of(gl * nq, nq)
        pat = jnp.where(grp == 0, 0, jnp.where(grp == n_groups - 1, 2, 1))
        return base, qcol, pat

    def put_scores(slot, geo, hh):
        base, qcol, _ = geo
        pair, sub = divmod(hh, 2)
        qh = qt_ref[hh * HEAD_DIM:(hh + 1) * HEAD_DIM, pl.ds(qcol, nq)]
        qp = jnp.concatenate([qh, zeros] if sub == 0 else [zeros, qh], axis=0)
        half = nk // MXU_COUNT
        for part in range(MXU_COUNT):
            kb = kband[pl.ds(base + part * half, half), pair * LANES:(pair + 1) * LANES]
            s_scr[slot, part * half:(part + 1) * half, :] = jnp.dot(kb, qp, preferred_element_type=F32)

    for ahead in range(SCORE_AHEAD):
        put_scores(ahead, geometry(0), ahead)

    def group_body(gl, carry):
        geo = geometry(gl)
        geo_next = geometry(jnp.minimum(gl + 1, n_local - 1))
        base, qcol, pat = geo
        outs = []
        for hh in range(N_HEADS_B):
            ahead = hh + SCORE_AHEAD
            nxt = (geo, ahead) if ahead < N_HEADS_B else (geo_next, ahead - N_HEADS_B)
            put_scores(ahead % SCORE_SLOTS, *nxt)
            s = s_scr[hh % SCORE_SLOTS] + bias_ref[hh * 3 + pat]
            m = jnp.max(s, axis=0, keepdims=True)
            eb = jnp.exp2(s - m).astype(BF16)
            pv = None
            for c in range(nk // MXU_TILE):
                vt = vband[hh * V_ROWS:(hh + 1) * V_ROWS, pl.ds(base + c * MXU_TILE, MXU_TILE)]
                d = jnp.dot(vt, eb[c * MXU_TILE:(c + 1) * MXU_TILE], preferred_element_type=F32)
                pv = d if pv is None else pv + d
            outs.append(pv[:HEAD_DIM] / pv[HEAD_DIM:HEAD_DIM + 1])
            if hh % 2 == 1:
                o_pair = jnp.concatenate(outs, axis=0)
                o_ref[pl.ds(qcol, nq), (hh // 2) * LANES:(hh // 2 + 1) * LANES] = o_pair.T.astype(BF16)
                outs = []
        return carry

    lax.fori_loop(0, n_local, group_body, 0)


def _na_bias_table(rpb):
    c = np.arange(GRID_W)
    kc, qc = c[:, None], c[None, :]
    col_start = np.clip(qc - NA_COLS // 2, 0, GRID_W - NA_COLS)
    in_win = (kc >= col_start) & (kc < col_start + NA_COLS)
    col_idx = np.clip(kc - qc, -(NA_COLS - 1), NA_COLS - 1) + NA_COLS - 1
    cols = jnp.where(jnp.asarray(in_win), rpb[:, :, col_idx].astype(F32) * LOG2E, -jnp.inf)
    outside = jnp.full((N_HEADS_B, GRID_W, GRID_W), -jnp.inf, F32)
    q_pos = (0, NA_MAX_ROWS // 2, NA_MAX_ROWS)
    patterns = []
    for p in range(3):
        key_rows = []
        for kr in range(NA_BAND_ROWS):
            blocks = []
            for qi in range(NA_GROUP_ROWS):
                w_start = (0, qi, NA_MAX_ROWS // 2)[p]
                inside = w_start <= kr < w_start + NA_MAX_ROWS
                blocks.append(cols[:, kr - (q_pos[p] + qi) + NA_MAX_ROWS - 1] if inside else outside)
            key_rows.append(jnp.concatenate(blocks, axis=-1))
        patterns.append(jnp.concatenate(key_rows, axis=1))
    tab = jnp.stack(patterns, axis=1)
    return tab.reshape(N_HEADS_B * 3, NA_BAND_ROWS * GRID_W, NA_GROUP_ROWS * GRID_W)


def _na_attention(qbt, kb, vbt, bias):
    B, S, W = kb.shape
    rows = S // GRID_W
    assert rows >= NA_BAND_ROWS and rows % NA_MAX_ROWS == 0
    rb = min(NA_BLOCK_ROWS, rows)
    halo = NA_MAX_ROWS * GRID_W
    per = rb // NA_MAX_ROWS
    last = rows // NA_MAX_ROWS - 1
    prev_idx = lambda i: jnp.maximum(i * per - 1, 0)
    next_idx = lambda i: jnp.minimum((i + 1) * per, last)
    k_main = pl.BlockSpec((None, rb * GRID_W, W), lambda b, i: (b, i, 0))
    k_prev = pl.BlockSpec((None, halo, W), lambda b, i: (b, prev_idx(i), 0))
    k_next = pl.BlockSpec((None, halo, W), lambda b, i: (b, next_idx(i), 0))
    q_main = pl.BlockSpec((None, W, rb * GRID_W), lambda b, i: (b, 0, i))
    vw = N_HEADS_B * V_ROWS
    v_main = pl.BlockSpec((None, vw, rb * GRID_W), lambda b, i: (b, 0, i))
    v_prev = pl.BlockSpec((None, vw, halo), lambda b, i: (b, 0, prev_idx(i)))
    v_next = pl.BlockSpec((None, vw, halo), lambda b, i: (b, 0, next_idx(i)))
    kern = functools.partial(_na_kernel, rows=rows, rb=rb)
    band_tokens = (rb + 2 * NA_MAX_ROWS) * GRID_W
    return pl.pallas_call(
        kern,
        grid=(B, rows // rb),
        in_specs=[q_main, k_prev, k_main, k_next, v_prev, v_main, v_next, _const_spec(bias.shape)],
        out_specs=k_main,
        out_shape=jax.ShapeDtypeStruct((B, S, W), BF16),
        scratch_shapes=[pltpu.VMEM((band_tokens, W), BF16), pltpu.VMEM((vw, band_tokens), BF16),
                        pltpu.VMEM((SCORE_SLOTS, NA_BAND_ROWS * GRID_W, NA_GROUP_ROWS * GRID_W), F32)],
        compiler_params=_params(2, 56),
        name="natten",
    )(qbt, kb, kb, kb, vbt, vbt, vbt, bias)


def _merge_kernel(x_ref, oa_ref, ob_ref, nw_ref, wg_ref, bg_ref, wpa_ref, wpb_ref, wout_ref, pmn_ref, o_ref):
    x = x_ref[...]
    h = _rms(x, nw_ref[...]).astype(BF16)
    g = jnp.dot(h, wg_ref[...], preferred_element_type=F32) + bg_ref[...]
    gate = jax.nn.sigmoid(g)
    pa = jnp.dot(oa_ref[...], wpa_ref[...], preferred_element_type=F32)
    pb = jnp.dot(ob_ref[...], wpb_ref[...], preferred_element_type=F32)
    mix = gate[:, :D_MODEL] * pa + gate[:, D_MODEL:] * pb
    mo = jnp.dot(mix.astype(BF16), wout_ref[...], preferred_element_type=F32)
    o_ref[...] = x + _rms(mo, pmn_ref[...])


def _merge(x2, oa2, ob2, nw, wg, bg, wpa, wpb, wout, pmn):
    n, D = x2.shape
    tm = min(PROJ_TILE, n)
    row = lambda w: pl.BlockSpec((tm, w), lambda i: (i, 0))
    return pl.pallas_call(
        _merge_kernel,
        grid=(n // tm,),
        in_specs=[row(D), row(WIDTH_A), row(WIDTH_B), _const_spec((1, D)), _const_spec(wg.shape),
                  _const_spec((1, 2 * D)), _const_spec(wpa.shape), _const_spec(wpb.shape),
                  _const_spec(wout.shape), _const_spec((1, D))],
        out_specs=row(D),
        out_shape=jax.ShapeDtypeStruct((n, D), F32),
        compiler_params=_params(1, 40),
        name="merge",
    )(x2, oa2, ob2, nw, wg, bg, wpa, wpb, wout, pmn)


def _ffn_kernel(xm_ref, xp_ref, xn_ref, nw_ref, wup_ref, cw_ref, cb_ref, wdn_ref, pfn_ref, o_ref,
                u_scr, f_scr, *, tiles_per_seq):
    tm = xm_ref.shape[0]
    j = pl.program_id(0) % tiles_per_seq
    nw = nw_ref[...]
    xm = xm_ref[...]
    hp = _rms(xp_ref[...], nw) * (j > 0).astype(F32)
    hn = _rms(xn_ref[...], nw) * (j < tiles_per_seq - 1).astype(F32)
    h_ext = jnp.concatenate([hp, _rms(xm, nw), hn], axis=0).astype(BF16)
    for c in range(D_FF // FFN_CHUNK):
        conv = []
        for part in range(2):
            lo = part * D_FF + c * FFN_CHUNK
            cols = slice(lo, lo + FFN_CHUNK)
            u_scr[part] = jnp.dot(h_ext, wup_ref[:, cols], preferred_element_type=F32)
            acc = u_scr[part, pl.ds(HALO - 1, tm), :] * cw_ref[0:1, cols] + cb_ref[:, cols]
            acc = acc + u_scr[part, pl.ds(HALO, tm), :] * cw_ref[1:2, cols]
            acc = acc + u_scr[part, pl.ds(HALO + 1, tm), :] * cw_ref[2:3, cols]
            conv.append(acc)
        f = jax.nn.gelu(conv[0], approximate=True) * conv[1]
        f_scr[:, c * FFN_CHUNK:(c + 1) * FFN_CHUNK] = f.astype(BF16)
    y = jnp.dot(f_scr[...], wdn_ref[...], preferred_element_type=F32)
    o_ref[...] = xm + _rms(y, pfn_ref[...])


def _ffn(x2, seq_len, nw, wup, cw, cb, wdn, pfn):
    n, D = x2.shape
    tm = min(PROJ_TILE, seq_len)
    per = tm // HALO
    last = n // HALO - 1
    kern = functools.partial(_ffn_kernel, tiles_per_seq=seq_len // tm)
    return pl.pallas_call(
        kern,
        grid=(n // tm,),
        in_specs=[pl.BlockSpec((tm, D), lambda i: (i, 0)),
                  pl.BlockSpec((HALO, D), lambda i: (jnp.maximum(i * per - 1, 0), 0)),
                  pl.BlockSpec((HALO, D), lambda i: (jnp.minimum((i + 1) * per, last), 0)),
                  _const_spec((1, D)), _const_spec(wup.shape), _const_spec(cw.shape), _const_spec(cb.shape),
                  _const_spec(wdn.shape), _const_spec((1, D))],
        out_specs=pl.BlockSpec((tm, D), lambda i: (i, 0)),
        out_shape=jax.ShapeDtypeStruct((n, D), F32),
        scratch_shapes=[pltpu.VMEM((2, tm + 2 * HALO, FFN_CHUNK), F32), pltpu.VMEM((tm, D_FF), BF16)],
        compiler_params=_params(1, 48),
        name="ffn",
    )(x2, x2, x2, nw, wup, cw, cb, wdn, pfn)


def _rope_tables_t(seq_len):
    t = np.arange(seq_len)
    row = (t // GRID_W).astype(np.float32)
    col = (t % GRID_W).astype(np.float32)
    half = HEAD_DIM // 2
    freqs = (ROPE_THETA ** (-np.arange(0, half, 2, dtype=np.float32) / half)).astype(np.float32)
    ang_r = row[:, None] * freqs[None, :]
    ang_c = col[:, None] * freqs[None, :]
    ang = np.concatenate([ang_r, ang_r, ang_c, ang_c], axis=-1).astype(np.float32)
    return jnp.asarray(np.cos(ang).T, F32), jnp.asarray(np.sin(ang).T, F32)


def _prepare_weights(pre_mix_norm, w_in, b_gate, q_norm, k_norm, rpb, w_proj_a, w_proj_b, w_out,
                     post_mix_norm, pre_ffn_norm, w_up, conv_w, conv_b, w_down, post_ffn_norm):
    row = lambda v: v.reshape(1, -1).astype(F32)
    kb_lo = WIDTH_QKV_A + WIDTH_B
    kb_hi = kb_lo + WIDTH_B
    gate_lo = WIDTH_QKV_A + WIDTH_QKV_B
    return dict(
        nw1=row(pre_mix_norm),
        wt=jnp.concatenate([w_in[:, :kb_lo], w_in[:, kb_hi:gate_lo]], axis=1).T.astype(BF16),
        wkb=w_in[:, kb_lo:kb_hi].astype(BF16),
        wg=w_in[:, gate_lo:].astype(BF16),
        bg=row(b_gate),
        qn=q_norm.reshape(HEAD_DIM, 1).astype(F32),
        kn=k_norm.reshape(HEAD_DIM, 1).astype(F32),
        bias=_na_bias_table(rpb),
        wpa=w_proj_a.astype(BF16), wpb=w_proj_b.astype(BF16), wout=w_out.astype(BF16),
        pmn=row(post_mix_norm), nw2=row(pre_ffn_norm),
        wup=w_up.astype(BF16), cw=conv_w.astype(F32), cb=row(conv_b),
        wdn=w_down.astype(BF16), pfn=row(post_ffn_norm),
    )


def _encoder_layer(x, w):
    B, S, D = x.shape
    cos_t, sin_t = _rope_tables_t(S)
    qt, k, vt, qbt, kb, vbt = _project(x, w["nw1"], w["wt"], w["wkb"], cos_t, sin_t, w["qn"], w["kn"])
    oa = _gqa_attention(qt, k, vt)
    ob = _na_attention(qbt, kb, vbt, w["bias"])
    x2 = x.reshape(B * S, D)
    x2 = _merge(x2, oa.reshape(B * S, WIDTH_A), ob.reshape(B * S, WIDTH_B), w["nw1"], w["wg"], w["bg"],
                w["wpa"], w["wpb"], w["wout"], w["pmn"])
    x2 = _ffn(x2, S, w["nw2"], w["wup"], w["cw"], w["cb"], w["wdn"], w["pfn"])
    return x2.reshape(B, S, D)


def kernel(x_prompt, x_sample, pre_mix_norm, w_in, b_gate, q_norm, k_norm, rpb, w_proj_a, w_proj_b, w_out,
           post_mix_norm, pre_ffn_norm, w_up, conv_w, conv_b, w_down, post_ffn_norm):
    layers = [_prepare_weights(pre_mix_norm[l], w_in[l], b_gate[l], q_norm[l], k_norm[l], rpb[l],
                               w_proj_a[l], w_proj_b[l], w_out[l], post_mix_norm[l], pre_ffn_norm[l],
                               w_up[l], conv_w[l], conv_b[l], w_down[l], post_ffn_norm[l])
              for l in range(w_in.shape[0])]

    def run_trunk(x):
        for w in layers:
            x = _encoder_layer(x, w)
        return x

    return run_trunk(x_prompt), run_trunk(x_sample)
```

```python
import functools

import numpy as np
import jax
import jax.numpy as jnp
from jax import lax
from jax.experimental import pallas as pl
from jax.experimental.pallas import tpu as pltpu

D_MODEL = 1024
GRID_W = 64
HEAD_DIM = 64
N_HEADS_A = 8
N_KV_HEADS_A = 2
N_HEADS_B = 8
WIDTH_A = N_HEADS_A * HEAD_DIM
WIDTH_KV_A = N_KV_HEADS_A * HEAD_DIM
WIDTH_B = N_HEADS_B * HEAD_DIM
NA_MAX_ROWS = 8
NA_COLS = 16
ROPE_THETA = 10000.0
D_FF = 2816
CONV_WIDTH = 3
EPS = 1e-6
SCALE = HEAD_DIM ** -0.5
LOG2E = float(np.log2(np.e))
Q_SCALE = SCALE * LOG2E
GROUP_A = N_HEADS_A // N_KV_HEADS_A
WIDTH_QKV_A = WIDTH_A + 2 * WIDTH_KV_A
WIDTH_QKV_B = 3 * WIDTH_B

F32 = jnp.float32
BF16 = jnp.bfloat16
MIB = 1024 * 1024

SUBLANES = 8
LANES = 128
MXU_COUNT = 2
MXU_TILE = 256

ONES_ROWS = 16
V_ROWS = HEAD_DIM + ONES_ROWS

PROJ_TILE = 512
GQA_TQ = 256
GQA_TK = 512
GQA_UNROLL = 4
NA_BLOCK_ROWS = 32
NA_GROUP_ROWS = 4
NA_BAND_ROWS = 12
SCORE_SLOTS = 4
SCORE_AHEAD = 3
FFN_CHUNK = 256
HALO = SUBLANES
FFN_DOWN_PARTS = 4
GELU_C = float(np.sqrt(2.0 / np.pi))
GELU_K = 0.044715


def _const_spec(shape):
    n = len(shape)
    return pl.BlockSpec(shape, lambda *_: (0,) * n, pipeline_mode=pl.Buffered(1))


def _params(n_axes, vmem_mib):
    return pltpu.CompilerParams(dimension_semantics=("parallel",) * n_axes,
                                vmem_limit_bytes=vmem_mib * MIB)


def _rms(x, w):
    return x * lax.rsqrt(jnp.mean(x * x, axis=-1, keepdims=True) + EPS) * w


def _proj_kernel(x_ref, nw_ref, wt_ref, wkb_ref, cos_ref, sin_ref, qn_ref, kn_ref,
                 qt_ref, k_ref, vt_ref, qbt_ref, kb_ref, vbt_ref):
    h = _rms(x_ref[...], nw_ref[...]).astype(BF16)
    pt = lax.dot_general(wt_ref[...], h, (((1,), (1,)), ((), ())), preferred_element_type=F32)
    cos = cos_ref[...]
    sin = sin_ref[...]

    def norm_rope(t, w):
        t = t * lax.rsqrt(jnp.mean(t * t, axis=0, keepdims=True) + EPS) * w
        q4 = HEAD_DIM // 4
        rot = jnp.concatenate([-t[q4:2 * q4], t[0:q4], -t[3 * q4:4 * q4], t[2 * q4:3 * q4]], axis=0)
        return t * cos + rot * sin

    qn = qn_ref[...]
    kn = kn_ref[...]
    for hh in range(N_HEADS_A):
        lo = hh * HEAD_DIM
        qt_ref[lo:lo + HEAD_DIM, :] = (norm_rope(pt[lo:lo + HEAD_DIM], qn) * Q_SCALE).astype(BF16)
    kt = jnp.concatenate(
        [norm_rope(pt[WIDTH_A + g * HEAD_DIM:WIDTH_A + (g + 1) * HEAD_DIM], kn) for g in range(N_KV_HEADS_A)],
        axis=0)
    k_ref[...] = kt.T.astype(BF16)
    ones = jnp.ones((ONES_ROWS, pt.shape[1]), BF16)

    def put_values(ref, first_row, n_heads):
        for hh in range(n_heads):
            lo = first_row + hh * HEAD_DIM
            ref[hh * V_ROWS:hh * V_ROWS + HEAD_DIM, :] = pt[lo:lo + HEAD_DIM].astype(BF16)
            ref[hh * V_ROWS + HEAD_DIM:(hh + 1) * V_ROWS, :] = ones

    put_values(vt_ref, WIDTH_A + WIDTH_KV_A, N_KV_HEADS_A)
    qbt_ref[...] = (pt[WIDTH_QKV_A:WIDTH_QKV_A + WIDTH_B] * Q_SCALE).astype(BF16)
    put_values(vbt_ref, WIDTH_QKV_A + WIDTH_B, N_HEADS_B)
    kb_ref[...] = jnp.dot(h, wkb_ref[...], preferred_element_type=F32).astype(BF16)


def _project(x, nw, wt, wkb, cos_t, sin_t, qn, kn):
    B, S, D = x.shape
    tm = min(PROJ_TILE, S)
    grid = (B, S // tm)
    row_blk = lambda w: pl.BlockSpec((None, tm, w), lambda b, i: (b, i, 0))
    col_blk = lambda w: pl.BlockSpec((None, w, tm), lambda b, i: (b, 0, i))
    return pl.pallas_call(
        _proj_kernel,
        grid=grid,
        in_specs=[row_blk(D), _const_spec((1, D)), _const_spec(wt.shape), _const_spec(wkb.shape),
                  pl.BlockSpec((HEAD_DIM, tm), lambda b, i: (0, i)),
                  pl.BlockSpec((HEAD_DIM, tm), lambda b, i: (0, i)),
                  _const_spec((HEAD_DIM, 1)), _const_spec((HEAD_DIM, 1))],
        out_specs=[col_blk(WIDTH_A), row_blk(WIDTH_KV_A), col_blk(N_KV_HEADS_A * V_ROWS),
                   col_blk(WIDTH_B), row_blk(WIDTH_B), col_blk(N_HEADS_B * V_ROWS)],
        out_shape=[jax.ShapeDtypeStruct((B, WIDTH_A, S), BF16),
                   jax.ShapeDtypeStruct((B, S, WIDTH_KV_A), BF16),
                   jax.ShapeDtypeStruct((B, N_KV_HEADS_A * V_ROWS, S), BF16),
                   jax.ShapeDtypeStruct((B, WIDTH_B, S), BF16),
                   jax.ShapeDtypeStruct((B, S, WIDTH_B), BF16),
                   jax.ShapeDtypeStruct((B, N_HEADS_B * V_ROWS, S), BF16)],
        compiler_params=_params(2, 40),
        name="proj",
    )(x, nw, wt, wkb, cos_t, sin_t, qn, kn)


def _gqa_kernel(qt_ref, k_ref, vt_ref, o_ref, qp_ref, m_ref, acc_ref, s_scr, *, tk, n_kv):
    tq = qt_ref.shape[1]
    zeros = jnp.zeros((HEAD_DIM, tq), BF16)
    for hh in range(N_HEADS_A):
        g = hh // GROUP_A
        parts = [zeros] * N_KV_HEADS_A
        parts[g] = qt_ref[hh * HEAD_DIM:(hh + 1) * HEAD_DIM, :]
        qp_ref[hh] = jnp.concatenate(parts, axis=0)
    m_ref[...] = jnp.full(m_ref.shape, -jnp.inf, F32)
    acc_ref[...] = jnp.zeros(acc_ref.shape, F32)

    def put_scores(slot, j, hh):
        half = tk // MXU_COUNT
        for part in range(MXU_COUNT):
            rows = pl.ds(pl.multiple_of(j * tk + part * half, half), half)
            s_scr[slot, part * half:(part + 1) * half, :] = jnp.dot(
                k_ref[rows, :], qp_ref[hh], preferred_element_type=F32)

    for ahead in range(SCORE_AHEAD):
        put_scores(ahead, 0, ahead)

    def body(j, carry):
        start = pl.multiple_of(j * tk, tk)
        j_next = jnp.minimum(j + 1, n_kv - 1)
        for hh in range(N_HEADS_A):
            g = hh // GROUP_A
            ahead = hh + SCORE_AHEAD
            nxt = (j, ahead) if ahead < N_HEADS_A else (j_next, ahead - N_HEADS_A)
            put_scores(ahead % SCORE_SLOTS, *nxt)
            m_old = m_ref[hh]
            m_new = jnp.maximum(m_old, jnp.max(s_scr[hh % SCORE_SLOTS], axis=0, keepdims=True))
            alpha = jnp.exp2(m_old - m_new)
            pb = jnp.exp2(s_scr[hh % SCORE_SLOTS] - m_new).astype(BF16)
            m_ref[hh] = m_new
            pv = None
            for c in range(tk // MXU_TILE):
                vc = vt_ref[g * V_ROWS:(g + 1) * V_ROWS, pl.ds(start + c * MXU_TILE, MXU_TILE)]
                d = jnp.dot(vc, pb[c * MXU_TILE:(c + 1) * MXU_TILE], preferred_element_type=F32)
                pv = d if pv is None else pv + d
            acc_ref[hh] = alpha * acc_ref[hh] + pv
        return carry

    lax.fori_loop(0, n_kv, body, 0, unroll=min(GQA_UNROLL, max(n_kv // 2, 1)))
    for pair in range(N_HEADS_A // 2):
        o_pair = jnp.concatenate([acc_ref[hh, :HEAD_DIM] / acc_ref[hh, HEAD_DIM:HEAD_DIM + 1]
                                  for hh in (2 * pair, 2 * pair + 1)], axis=0)
        o_ref[:, pair * LANES:(pair + 1) * LANES] = o_pair.T.astype(BF16)


def _gqa_attention(qt, k, vt):
    B, _, S = qt.shape
    tq = min(GQA_TQ, S)
    tk = min(GQA_TK, S)
    kern = functools.partial(_gqa_kernel, tk=tk, n_kv=S // tk)
    return pl.pallas_call(
        kern,
        grid=(B, S // tq),
        in_specs=[pl.BlockSpec((None, WIDTH_A, tq), lambda b, i: (b, 0, i)),
                  pl.BlockSpec((None, S, WIDTH_KV_A), lambda b, i: (b, 0, 0)),
                  pl.BlockSpec((None, N_KV_HEADS_A * V_ROWS, S), lambda b, i: (b, 0, 0))],
        out_specs=pl.BlockSpec((None, tq, WIDTH_A), lambda b, i: (b, i, 0)),
        out_shape=jax.ShapeDtypeStruct((B, S, WIDTH_A), BF16),
        scratch_shapes=[pltpu.VMEM((N_HEADS_A, WIDTH_KV_A, tq), BF16),
                        pltpu.VMEM((N_HEADS_A, 1, tq), F32),
                        pltpu.VMEM((N_HEADS_A, V_ROWS, tq), F32),
                        pltpu.VMEM((SCORE_SLOTS, tk, tq), F32)],
        compiler_params=_params(2, 40),
        name="gqa",
    )(qt, k, vt)


def _na_kernel(qt_ref, kp_ref, km_ref, kn_ref, vp_ref, vm_ref, vn_ref, bias_ref, o_ref,
               kband, vband, s_scr, *, rows, rb):
    i = pl.program_id(1)
    halo = NA_MAX_ROWS * GRID_W
    main = rb * GRID_W
    kband[0:halo, :] = kp_ref[...]
    kband[halo:halo + main, :] = km_ref[...]
    kband[halo + main:, :] = kn_ref[...]
    vband[:, 0:halo] = vp_ref[...]
    vband[:, halo:halo + main] = vm_ref[...]
    vband[:, halo + main:] = vn_ref[...]
    nq = NA_GROUP_ROWS * GRID_W
    nk = NA_BAND_ROWS * GRID_W
    zeros = jnp.zeros((HEAD_DIM, nq), BF16)
    n_groups = rows // NA_GROUP_ROWS

    n_local = rb // NA_GROUP_ROWS

    def geometry(gl):
        grp = i * n_local + gl
        band_start = jnp.clip(grp * NA_GROUP_ROWS - NA_MAX_ROWS // 2, 0, rows - NA_BAND_ROWS)
        base = pl.multiple_of((band_start - i * rb + NA_MAX_ROWS) * GRID_W, nq)
        qcol = pl.multiple_of(gl * nq, nq)
        pat = jnp.where(grp == 0, 0, jnp.where(grp == n_groups - 1, 2, 1))
        return base, qcol, pat

    def put_scores(slot, geo, hh):
        base, qcol, _ = geo
        pair, sub = divmod(hh, 2)
        qh = qt_ref[hh * HEAD_DIM:(hh + 1) * HEAD_DIM, pl.ds(qcol, nq)]
        qp = jnp.concatenate([qh, zeros] if sub == 0 else [zeros, qh], axis=0)
        half = nk // MXU_COUNT
        for part in range(MXU_COUNT):
            kb = kband[pl.ds(base + part * half, half), pair * LANES:(pair + 1) * LANES]
            s_scr[slot, part * half:(part + 1) * half, :] = jnp.dot(kb, qp, preferred_element_type=F32)

    for ahead in range(SCORE_AHEAD):
        put_scores(ahead, geometry(0), ahead)

    def group_body(gl, carry):
        geo = geometry(gl)
        geo_next = geometry(jnp.minimum(gl + 1, n_local - 1))
        base, qcol, pat = geo
        outs = []
        for hh in range(N_HEADS_B):
            ahead = hh + SCORE_AHEAD
            nxt = (geo, ahead) if ahead < N_HEADS_B else (geo_next, ahead - N_HEADS_B)
            put_scores(ahead % SCORE_SLOTS, *nxt)
            s = s_scr[hh % SCORE_SLOTS] + bias_ref[hh * 3 + pat]
            m = jnp.max(s, axis=0, keepdims=True)
            eb = jnp.exp2(s - m).astype(BF16)
            pv = None
            for c in range(nk // MXU_TILE):
                vt = vband[hh * V_ROWS:(hh + 1) * V_ROWS, pl.ds(base + c * MXU_TILE, MXU_TILE)]
                d = jnp.dot(vt, eb[c * MXU_TILE:(c + 1) * MXU_TILE], preferred_element_type=F32)
                pv = d if pv is None else pv + d
            outs.append(pv[:HEAD_DIM] / pv[HEAD_DIM:HEAD_DIM + 1])
            if hh % 2 == 1:
                o_pair = jnp.concatenate(outs, axis=0)
                o_ref[pl.ds(qcol, nq), (hh // 2) * LANES:(hh // 2 + 1) * LANES] = o_pair.T.astype(BF16)
                outs = []
        return carry

    lax.fori_loop(0, n_local, group_body, 0)


def _na_bias_table(rpb):
    c = np.arange(GRID_W)
    kc, qc = c[:, None], c[None, :]
    col_start = np.clip(qc - NA_COLS // 2, 0, GRID_W - NA_COLS)
    in_win = (kc >= col_start) & (kc < col_start + NA_COLS)
    col_idx = np.clip(kc - qc, -(NA_COLS - 1), NA_COLS - 1) + NA_COLS - 1
    cols = jnp.where(jnp.asarray(in_win), rpb[:, :, col_idx].astype(F32) * LOG2E, -jnp.inf)
    outside = jnp.full((N_HEADS_B, GRID_W, GRID_W), -jnp.inf, F32)
    q_pos = (0, NA_MAX_ROWS // 2, NA_MAX_ROWS)
    patterns = []
    for p in range(3):
        key_rows = []
        for kr in range(NA_BAND_ROWS):
            blocks = []
            for qi in range(NA_GROUP_ROWS):
                w_start = (0, qi, NA_MAX_ROWS // 2)[p]
                inside = w_start <= kr < w_start + NA_MAX_ROWS
                blocks.append(cols[:, kr - (q_pos[p] + qi) + NA_MAX_ROWS - 1] if inside else outside)
            key_rows.append(jnp.concatenate(blocks, axis=-1))
        patterns.append(jnp.concatenate(key_rows, axis=1))
    tab = jnp.stack(patterns, axis=1)
    return tab.reshape(N_HEADS_B * 3, NA_BAND_ROWS * GRID_W, NA_GROUP_ROWS * GRID_W)


def _na_attention(qbt, kb, vbt, bias):
    B, S, W = kb.shape
    rows = S // GRID_W
    assert rows >= NA_BAND_ROWS and rows % NA_MAX_ROWS == 0
    rb = min(NA_BLOCK_ROWS, rows)
    halo = NA_MAX_ROWS * GRID_W
    per = rb // NA_MAX_ROWS
    last = rows // NA_MAX_ROWS - 1
    prev_idx = lambda i: jnp.maximum(i * per - 1, 0)
    next_idx = lambda i: jnp.minimum((i + 1) * per, last)
    k_main = pl.BlockSpec((None, rb * GRID_W, W), lambda b, i: (b, i, 0))
    k_prev = pl.BlockSpec((None, halo, W), lambda b, i: (b, prev_idx(i), 0))
    k_next = pl.BlockSpec((None, halo, W), lambda b, i: (b, next_idx(i), 0))
    q_main = pl.BlockSpec((None, W, rb * GRID_W), lambda b, i: (b, 0, i))
    vw = N_HEADS_B * V_ROWS
    v_main = pl.BlockSpec((None, vw, rb * GRID_W), lambda b, i: (b, 0, i))
    v_prev = pl.BlockSpec((None, vw, halo), lambda b, i: (b, 0, prev_idx(i)))
    v_next = pl.BlockSpec((None, vw, halo), lambda b, i: (b, 0, next_idx(i)))
    kern = functools.partial(_na_kernel, rows=rows, rb=rb)
    band_tokens = (rb + 2 * NA_MAX_ROWS) * GRID_W
    return pl.pallas_call(
        kern,
        grid=(B, rows // rb),
        in_specs=[q_main, k_prev, k_main, k_next, v_prev, v_main, v_next, _const_spec(bias.shape)],
        out_specs=k_main,
        out_shape=jax.ShapeDtypeStruct((B, S, W), BF16),
        scratch_shapes=[pltpu.VMEM((band_tokens, W), BF16), pltpu.VMEM((vw, band_tokens), BF16),
                        pltpu.VMEM((SCORE_SLOTS, NA_BAND_ROWS * GRID_W, NA_GROUP_ROWS * GRID_W), F32)],
        compiler_params=_params(2, 56),
        name="natten",
    )(qbt, kb, kb, kb, vbt, vbt, vbt, bias)


def _merge_kernel(x_ref, oa_ref, ob_ref, nw_ref, wg_ref, bg_ref, wpa_ref, wpb_ref, wout_ref, pmn_ref, o_ref):
    x = x_ref[...]
    h = _rms(x, nw_ref[...]).astype(BF16)
    g = jnp.dot(h, wg_ref[...], preferred_element_type=F32) + bg_ref[...]
    gate = jax.nn.sigmoid(g)
    pa = jnp.dot(oa_ref[...], wpa_ref[...], preferred_element_type=F32)
    pb = jnp.dot(ob_ref[...], wpb_ref[...], preferred_element_type=F32)
    mix = gate[:, :D_MODEL] * pa + gate[:, D_MODEL:] * pb
    mo = jnp.dot(mix.astype(BF16), wout_ref[...], preferred_element_type=F32)
    o_ref[...] = x + _rms(mo, pmn_ref[...])


def _merge(x2, oa2, ob2, nw, wg, bg, wpa, wpb, wout, pmn):
    n, D = x2.shape
    tm = min(PROJ_TILE, n)
    row = lambda w: pl.BlockSpec((tm, w), lambda i: (i, 0))
    return pl.pallas_call(
        _merge_kernel,
        grid=(n // tm,),
        in_specs=[row(D), row(WIDTH_A), row(WIDTH_B), _const_spec((1, D)), _const_spec(wg.shape),
                  _const_spec((1, 2 * D)), _const_spec(wpa.shape), _const_spec(wpb.shape),
                  _const_spec(wout.shape), _const_spec((1, D))],
        out_specs=row(D),
        out_shape=jax.ShapeDtypeStruct((n, D), F32),
        compiler_params=_params(1, 40),
        name="merge",
    )(x2, oa2, ob2, nw, wg, bg, wpa, wpb, wout, pmn)


def _gelu_tanh(x):
    inner = x * (GELU_C + (GELU_C * GELU_K) * (x * x))
    return x * (0.5 + 0.5 * jnp.tanh(inner))


def _ffn_kernel(xm_ref, xp_ref, xn_ref, nw_ref, wup_ref, cw_ref, cb_ref, wdn_ref, pfn_ref, o_ref,
                f_scr, *, tiles_per_seq):
    tm = xm_ref.shape[0]
    ext = tm + 2 * HALO
    j = pl.program_id(0) % tiles_per_seq
    nw = nw_ref[...]
    xm = xm_ref[...]
    hp = _rms(xp_ref[...], nw) * (j > 0).astype(F32)
    hn = _rms(xn_ref[...], nw) * (j < tiles_per_seq - 1).astype(F32)
    h_ext = jnp.concatenate([hp, _rms(xm, nw), hn], axis=0).astype(BF16)
    main = slice(HALO, HALO + tm)

    def up(c):
        cols = [slice(half + c * FFN_CHUNK, half + (c + 1) * FFN_CHUNK) for half in (0, D_FF)]
        return [(jnp.dot(h_ext, wup_ref[:, cs], preferred_element_type=F32), cs) for cs in cols]

    def conv(u, cols):
        before = pltpu.roll(u, 1, 0)[main]
        after = pltpu.roll(u, ext - 1, 0)[main]
        acc = before * cw_ref[0:1, cols] + cb_ref[:, cols]
        acc = acc + u[main] * cw_ref[1:2, cols]
        return acc + after * cw_ref[2:3, cols]

    def down(part):
        rows = slice(part[0] * FFN_CHUNK, (part[-1] + 1) * FFN_CHUNK)
        return jnp.dot(f_scr[:, rows], wdn_ref[rows, :], preferred_element_type=F32)

    n_chunks = D_FF // FFN_CHUNK
    parts = [[int(c) for c in p] for p in np.array_split(np.arange(n_chunks), FFN_DOWN_PARTS)]
    part_ending_at = {p[-1]: p for p in parts}
    y = None
    ready = None
    u_next = up(0)
    for c in range(n_chunks):
        (gate_u, gate_cols), (val_u, val_cols) = u_next
        if c + 1 < n_chunks:
            u_next = up(c + 1)
        if ready is not None:
            d = down(ready)
            y = d if y is None else y + d
        f = _gelu_tanh(conv(gate_u, gate_cols)) * conv(val_u, val_cols)
        f_scr[:, c * FFN_CHUNK:(c + 1) * FFN_CHUNK] = f.astype(BF16)
        ready = part_ending_at.get(c)
    o_ref[...] = xm + _rms(y + down(ready), pfn_ref[...])


def _ffn(x2, seq_len, nw, wup, cw, cb, wdn, pfn):
    n, D = x2.shape
    tm = min(PROJ_TILE, seq_len)
    per = tm // HALO
    last = n // HALO - 1
    kern = functools.partial(_ffn_kernel, tiles_per_seq=seq_len // tm)
    return pl.pallas_call(
        kern,
        grid=(n // tm,),
        in_specs=[pl.BlockSpec((tm, D), lambda i: (i, 0)),
                  pl.BlockSpec((HALO, D), lambda i: (jnp.maximum(i * per - 1, 0), 0)),
                  pl.BlockSpec((HALO, D), lambda i: (jnp.minimum((i + 1) * per, last), 0)),
                  _const_spec((1, D)), _const_spec(wup.shape), _const_spec(cw.shape), _const_spec(cb.shape),
                  _const_spec(wdn.shape), _const_spec((1, D))],
        out_specs=pl.BlockSpec((tm, D), lambda i: (i, 0)),
        out_shape=jax.ShapeDtypeStruct((n, D), F32),
        scratch_shapes=[pltpu.VMEM((tm, D_FF), BF16)],
        compiler_params=_params(1, 48),
        name="ffn",
    )(x2, x2, x2, nw, wup, cw, cb, wdn, pfn)


def _rope_tables_t(seq_len):
    t = np.arange(seq_len)
    row = (t // GRID_W).astype(np.float32)
    col = (t % GRID_W).astype(np.float32)
    half = HEAD_DIM // 2
    freqs = (ROPE_THETA ** (-np.arange(0, half, 2, dtype=np.float32) / half)).astype(np.float32)
    ang_r = row[:, None] * freqs[None, :]
    ang_c = col[:, None] * freqs[None, :]
    ang = np.concatenate([ang_r, ang_r, ang_c, ang_c], axis=-1).astype(np.float32)
    return jnp.asarray(np.cos(ang).T, F32), jnp.asarray(np.sin(ang).T, F32)


def _prepare_weights(pre_mix_norm, w_in, b_gate, q_norm, k_norm, rpb, w_proj_a, w_proj_b, w_out,
                     post_mix_norm, pre_ffn_norm, w_up, conv_w, conv_b, w_down, post_ffn_norm):
    row = lambda v: v.reshape(1, -1).astype(F32)
    kb_lo = WIDTH_QKV_A + WIDTH_B
    kb_hi = kb_lo + WIDTH_B
    gate_lo = WIDTH_QKV_A + WIDTH_QKV_B
    return dict(
        nw1=row(pre_mix_norm),
        wt=jnp.concatenate([w_in[:, :kb_lo], w_in[:, kb_hi:gate_lo]], axis=1).T.astype(BF16),
        wkb=w_in[:, kb_lo:kb_hi].astype(BF16),
        wg=w_in[:, gate_lo:].astype(BF16),
        bg=row(b_gate),
        qn=q_norm.reshape(HEAD_DIM, 1).astype(F32),
        kn=k_norm.reshape(HEAD_DIM, 1).astype(F32),
        bias=_na_bias_table(rpb),
        wpa=w_proj_a.astype(BF16), wpb=w_proj_b.astype(BF16), wout=w_out.astype(BF16),
        pmn=row(post_mix_norm), nw2=row(pre_ffn_norm),
        wup=w_up.astype(BF16), cw=conv_w.astype(F32), cb=row(conv_b),
        wdn=w_down.astype(BF16), pfn=row(post_ffn_norm),
    )


def _encoder_layer(x, w):
    B, S, D = x.shape
    cos_t, sin_t = _rope_tables_t(S)
    qt, k, vt, qbt, kb, vbt = _project(x, w["nw1"], w["wt"], w["wkb"], cos_t, sin_t, w["qn"], w["kn"])
    oa = _gqa_attention(qt, k, vt)
    ob = _na_attention(qbt, kb, vbt, w["bias"])
    x2 = x.reshape(B * S, D)
    x2 = _merge(x2, oa.reshape(B * S, WIDTH_A), ob.reshape(B * S, WIDTH_B), w["nw1"], w["wg"], w["bg"],
                w["wpa"], w["wpb"], w["wout"], w["pmn"])
    x2 = _ffn(x2, S, w["nw2"], w["wup"], w["cw"], w["cb"], w["wdn"], w["pfn"])
    return x2.reshape(B, S, D)


def kernel(x_prompt, x_sample, pre_mix_norm, w_in, b_gate, q_norm, k_norm, rpb, w_proj_a, w_proj_b, w_out,
           post_mix_norm, pre_ffn_norm, w_up, conv_w, conv_b, w_down, post_ffn_norm):
    layers = [_prepare_weights(pre_mix_norm[l], w_in[l], b_gate[l], q_norm[l], k_norm[l], rpb[l],
                               w_proj_a[l], w_proj_b[l], w_out[l], post_mix_norm[l], pre_ffn_norm[l],
                               w_up[l], conv_w[l], conv_b[l], w_down[l], post_ffn_norm[l])
              for l in range(w_in.shape[0])]

    def run_trunk(x):
        for w in layers:
            x = _encoder_layer(x, w)
        return x

    return run_trunk(x_prompt), run_trunk(x_sample)
```

```python
import functools

import numpy as np
import jax
import jax.numpy as jnp
from jax import lax
from jax.experimental import pallas as pl
from jax.experimental.pallas import tpu as pltpu

D_MODEL = 1024
GRID_W = 64
HEAD_DIM = 64
N_HEADS_A = 8
N_KV_HEADS_A = 2
N_HEADS_B = 8
WIDTH_A = N_HEADS_A * HEAD_DIM
WIDTH_KV_A = N_KV_HEADS_A * HEAD_DIM
WIDTH_B = N_HEADS_B * HEAD_DIM
NA_MAX_ROWS = 8
NA_COLS = 16
ROPE_THETA = 10000.0
D_FF = 2816
CONV_WIDTH = 3
EPS = 1e-6
SCALE = HEAD_DIM ** -0.5
LOG2E = float(np.log2(np.e))
Q_SCALE = SCALE * LOG2E
GROUP_A = N_HEADS_A // N_KV_HEADS_A
WIDTH_QKV_A = WIDTH_A + 2 * WIDTH_KV_A
WIDTH_QKV_B = 3 * WIDTH_B

F32 = jnp.float32
BF16 = jnp.bfloat16
MIB = 1024 * 1024

SUBLANES = 8
LANES = 128
MXU_COUNT = 2
MXU_TILE = 256

ONES_ROWS = 16
V_ROWS = HEAD_DIM + ONES_ROWS

PROJ_TILE = 512
FFN_TILE = 512
GQA_TQ = 512
GQA_TK = 256
GQA_UNROLL = 4
NA_BLOCK_ROWS = 32
NA_GROUP_ROWS = 4
NA_BAND_ROWS = 12
NA_UNROLL = 2
SCORE_SLOTS = 4
SCORE_AHEAD = 3
FFN_CHUNK = 256
HALO = SUBLANES
FFN_DOWN_PARTS = 4
GELU_C = float(np.sqrt(2.0 / np.pi))
GELU_K = 0.044715


def _const_spec(shape):
    n = len(shape)
    return pl.BlockSpec(shape, lambda *_: (0,) * n, pipeline_mode=pl.Buffered(1))


def _params(n_axes, vmem_mib):
    return pltpu.CompilerParams(dimension_semantics=("parallel",) * n_axes,
                                vmem_limit_bytes=vmem_mib * MIB)


def _rms(x, w):
    return x * lax.rsqrt(jnp.mean(x * x, axis=-1, keepdims=True) + EPS) * w


def _proj_kernel(x_ref, nw_ref, wt_ref, wkb_ref, cos_ref, sin_ref, qn_ref, kn_ref,
                 qt_ref, k_ref, vt_ref, qbt_ref, kb_ref, vbt_ref):
    h = _rms(x_ref[...], nw_ref[...]).astype(BF16)
    pt = lax.dot_general(wt_ref[...], h, (((1,), (1,)), ((), ())), preferred_element_type=F32)
    cos = cos_ref[...]
    sin = sin_ref[...]

    def norm_rope(t, w):
        t = t * lax.rsqrt(jnp.mean(t * t, axis=0, keepdims=True) + EPS) * w
        q4 = HEAD_DIM // 4
        rot = jnp.concatenate([-t[q4:2 * q4], t[0:q4], -t[3 * q4:4 * q4], t[2 * q4:3 * q4]], axis=0)
        return t * cos + rot * sin

    qn = qn_ref[...]
    kn = kn_ref[...]
    for hh in range(N_HEADS_A):
        lo = hh * HEAD_DIM
        qt_ref[lo:lo + HEAD_DIM, :] = (norm_rope(pt[lo:lo + HEAD_DIM], qn) * Q_SCALE).astype(BF16)
    kt = jnp.concatenate(
        [norm_rope(pt[WIDTH_A + g * HEAD_DIM:WIDTH_A + (g + 1) * HEAD_DIM], kn) for g in range(N_KV_HEADS_A)],
        axis=0)
    k_ref[...] = kt.T.astype(BF16)
    ones = jnp.ones((ONES_ROWS, pt.shape[1]), BF16)

    def put_values(ref, first_row, n_heads):
        for hh in range(n_heads):
            lo = first_row + hh * HEAD_DIM
            ref[hh * V_ROWS:hh * V_ROWS + HEAD_DIM, :] = pt[lo:lo + HEAD_DIM].astype(BF16)
            ref[hh * V_ROWS + HEAD_DIM:(hh + 1) * V_ROWS, :] = ones

    put_values(vt_ref, WIDTH_A + WIDTH_KV_A, N_KV_HEADS_A)
    qbt_ref[...] = (pt[WIDTH_QKV_A:WIDTH_QKV_A + WIDTH_B] * Q_SCALE).astype(BF16)
    put_values(vbt_ref, WIDTH_QKV_A + WIDTH_B, N_HEADS_B)
    kb_ref[...] = jnp.dot(h, wkb_ref[...], preferred_element_type=F32).astype(BF16)


def _project(x, nw, wt, wkb, cos_t, sin_t, qn, kn):
    B, S, D = x.shape
    tm = min(PROJ_TILE, S)
    grid = (B, S // tm)
    row_blk = lambda w: pl.BlockSpec((None, tm, w), lambda b, i: (b, i, 0))
    col_blk = lambda w: pl.BlockSpec((None, w, tm), lambda b, i: (b, 0, i))
    return pl.pallas_call(
        _proj_kernel,
        grid=grid,
        in_specs=[row_blk(D), _const_spec((1, D)), _const_spec(wt.shape), _const_spec(wkb.shape),
                  pl.BlockSpec((HEAD_DIM, tm), lambda b, i: (0, i)),
                  pl.BlockSpec((HEAD_DIM, tm), lambda b, i: (0, i)),
                  _const_spec((HEAD_DIM, 1)), _const_spec((HEAD_DIM, 1))],
        out_specs=[col_blk(WIDTH_A), row_blk(WIDTH_KV_A), col_blk(N_KV_HEADS_A * V_ROWS),
                   col_blk(WIDTH_B), row_blk(WIDTH_B), col_blk(N_HEADS_B * V_ROWS)],
        out_shape=[jax.ShapeDtypeStruct((B, WIDTH_A, S), BF16),
                   jax.ShapeDtypeStruct((B, S, WIDTH_KV_A), BF16),
                   jax.ShapeDtypeStruct((B, N_KV_HEADS_A * V_ROWS, S), BF16),
                   jax.ShapeDtypeStruct((B, WIDTH_B, S), BF16),
                   jax.ShapeDtypeStruct((B, S, WIDTH_B), BF16),
                   jax.ShapeDtypeStruct((B, N_HEADS_B * V_ROWS, S), BF16)],
        compiler_params=_params(2, 40),
        name="proj",
    )(x, nw, wt, wkb, cos_t, sin_t, qn, kn)


def _gqa_kernel(qt_ref, k_ref, vt_ref, o_ref, qp_ref, m_ref, acc_ref, s_scr, *, tk, n_kv):
    tq = qt_ref.shape[1]
    zeros = jnp.zeros((HEAD_DIM, tq), BF16)
    for hh in range(N_HEADS_A):
        g = hh // GROUP_A
        parts = [zeros] * N_KV_HEADS_A
        parts[g] = qt_ref[hh * HEAD_DIM:(hh + 1) * HEAD_DIM, :]
        qp_ref[hh] = jnp.concatenate(parts, axis=0)
    m_ref[...] = jnp.full(m_ref.shape, -jnp.inf, F32)
    acc_ref[...] = jnp.zeros(acc_ref.shape, F32)

    def put_scores(slot, j, hh):
        for kt in range(tk // MXU_TILE):
            rows = pl.ds(pl.multiple_of(j * tk + kt * MXU_TILE, MXU_TILE), MXU_TILE)
            for qt in range(tq // MXU_TILE):
                cols = slice(qt * MXU_TILE, (qt + 1) * MXU_TILE)
                s_scr[slot, kt * MXU_TILE:(kt + 1) * MXU_TILE, cols] = jnp.dot(
                    k_ref[rows, :], qp_ref[hh, :, cols], preferred_element_type=F32)

    for ahead in range(SCORE_AHEAD):
        put_scores(ahead, 0, ahead)

    def body(j, carry):
        start = pl.multiple_of(j * tk, tk)
        j_next = jnp.minimum(j + 1, n_kv - 1)
        for hh in range(N_HEADS_A):
            g = hh // GROUP_A
            ahead = hh + SCORE_AHEAD
            nxt = (j, ahead) if ahead < N_HEADS_A else (j_next, ahead - N_HEADS_A)
            put_scores(ahead % SCORE_SLOTS, *nxt)
            m_old = m_ref[hh]
            m_new = jnp.maximum(m_old, jnp.max(s_scr[hh % SCORE_SLOTS], axis=0, keepdims=True))
            alpha = jnp.exp2(m_old - m_new)
            pb = jnp.exp2(s_scr[hh % SCORE_SLOTS] - m_new).astype(BF16)
            m_ref[hh] = m_new
            pv_cols = []
            for qt in range(tq // MXU_TILE):
                cols = slice(qt * MXU_TILE, (qt + 1) * MXU_TILE)
                pv = None
                for kt in range(tk // MXU_TILE):
                    vc = vt_ref[g * V_ROWS:(g + 1) * V_ROWS, pl.ds(start + kt * MXU_TILE, MXU_TILE)]
                    d = jnp.dot(vc, pb[kt * MXU_TILE:(kt + 1) * MXU_TILE, cols], preferred_element_type=F32)
                    pv = d if pv is None else pv + d
                pv_cols.append(pv)
            pv = jnp.concatenate(pv_cols, axis=1)
            acc_ref[hh] = alpha * acc_ref[hh] + pv
        return carry

    lax.fori_loop(0, n_kv, body, 0, unroll=min(GQA_UNROLL, max(n_kv // 2, 1)))
    for pair in range(N_HEADS_A // 2):
        o_pair = jnp.concatenate([acc_ref[hh, :HEAD_DIM] / acc_ref[hh, HEAD_DIM:HEAD_DIM + 1]
                                  for hh in (2 * pair, 2 * pair + 1)], axis=0)
        o_ref[:, pair * LANES:(pair + 1) * LANES] = o_pair.T.astype(BF16)


def _gqa_attention(qt, k, vt):
    B, _, S = qt.shape
    tq = min(GQA_TQ, S)
    tk = min(GQA_TK, S)
    kern = functools.partial(_gqa_kernel, tk=tk, n_kv=S // tk)
    return pl.pallas_call(
        kern,
        grid=(B, S // tq),
        in_specs=[pl.BlockSpec((None, WIDTH_A, tq), lambda b, i: (b, 0, i)),
                  pl.BlockSpec((None, S, WIDTH_KV_A), lambda b, i: (b, 0, 0)),
                  pl.BlockSpec((None, N_KV_HEADS_A * V_ROWS, S), lambda b, i: (b, 0, 0))],
        out_specs=pl.BlockSpec((None, tq, WIDTH_A), lambda b, i: (b, i, 0)),
        out_shape=jax.ShapeDtypeStruct((B, S, WIDTH_A), BF16),
        scratch_shapes=[pltpu.VMEM((N_HEADS_A, WIDTH_KV_A, tq), BF16),
                        pltpu.VMEM((N_HEADS_A, 1, tq), F32),
                        pltpu.VMEM((N_HEADS_A, V_ROWS, tq), F32),
                        pltpu.VMEM((SCORE_SLOTS, tk, tq), F32)],
        compiler_params=_params(2, 40),
        name="gqa",
    )(qt, k, vt)


def _na_kernel(qt_ref, kp_ref, km_ref, kn_ref, vp_ref, vm_ref, vn_ref, bias_ref, o_ref,
               kband, vband, s_scr, *, rows, rb):
    i = pl.program_id(1)
    halo = NA_MAX_ROWS * GRID_W
    main = rb * GRID_W
    kband[0:halo, :] = kp_ref[...]
    kband[halo:halo + main, :] = km_ref[...]
    kband[halo + main:, :] = kn_ref[...]
    vband[:, 0:halo] = vp_ref[...]
    vband[:, halo:halo + main] = vm_ref[...]
    vband[:, halo + main:] = vn_ref[...]
    nq = NA_GROUP_ROWS * GRID_W
    nk = NA_BAND_ROWS * GRID_W
    zeros = jnp.zeros((HEAD_DIM, nq), BF16)
    n_groups = rows // NA_GROUP_ROWS

    n_local = rb // NA_GROUP_ROWS

    def geometry(gl):
        grp = i * n_local + gl
        band_start = jnp.clip(grp * NA_GROUP_ROWS - NA_MAX_ROWS // 2, 0, rows - NA_BAND_ROWS)
        base = pl.multiple_of((band_start - i * rb + NA_MAX_ROWS) * GRID_W, nq)
        qcol = pl.multiple_of(gl * nq, nq)
        pat = jnp.where(grp == 0, 0, jnp.where(grp == n_groups - 1, 2, 1))
        return base, qcol, pat

    def put_scores(slot, geo, hh):
        base, qcol, _ = geo
        pair, sub = divmod(hh, 2)
        qh = qt_ref[hh * HEAD_DIM:(hh + 1) * HEAD_DIM, pl.ds(qcol, nq)]
        qp = jnp.concatenate([qh, zeros] if sub == 0 else [zeros, qh], axis=0)
        half = nk // MXU_COUNT
        for part in range(MXU_COUNT):
            kb = kband[pl.ds(base + part * half, half), pair * LANES:(pair + 1) * LANES]
            s_scr[slot, part * half:(part + 1) * half, :] = jnp.dot(kb, qp, preferred_element_type=F32)

    for ahead in range(SCORE_AHEAD):
        put_scores(ahead, geometry(0), ahead)

    def group_body(gl, carry):
        geo = geometry(gl)
        geo_next = geometry(jnp.minimum(gl + 1, n_local - 1))
        base, qcol, pat = geo
        outs = []
        for hh in range(N_HEADS_B):
            ahead = hh + SCORE_AHEAD
            nxt = (geo, ahead) if ahead < N_HEADS_B else (geo_next, ahead - N_HEADS_B)
            put_scores(ahead % SCORE_SLOTS, *nxt)
            s = s_scr[hh % SCORE_SLOTS] + bias_ref[hh * 3 + pat]
            m = jnp.max(s, axis=0, keepdims=True)
            eb = jnp.exp2(s - m).astype(BF16)
            pv = None
            for c in range(nk // MXU_TILE):
                vt = vband[hh * V_ROWS:(hh + 1) * V_ROWS, pl.ds(base + c * MXU_TILE, MXU_TILE)]
                d = jnp.dot(vt, eb[c * MXU_TILE:(c + 1) * MXU_TILE], preferred_element_type=F32)
                pv = d if pv is None else pv + d
            outs.append(pv[:HEAD_DIM] / pv[HEAD_DIM:HEAD_DIM + 1])
            if hh % 2 == 1:
                o_pair = jnp.concatenate(outs, axis=0)
                o_ref[pl.ds(qcol, nq), (hh // 2) * LANES:(hh // 2 + 1) * LANES] = o_pair.T.astype(BF16)
                outs = []
        return carry

    lax.fori_loop(0, n_local, group_body, 0, unroll=min(NA_UNROLL, max(n_local // 2, 1)))


def _na_bias_table(rpb):
    c = np.arange(GRID_W)
    kc, qc = c[:, None], c[None, :]
    col_start = np.clip(qc - NA_COLS // 2, 0, GRID_W - NA_COLS)
    in_win = (kc >= col_start) & (kc < col_start + NA_COLS)
    col_idx = np.clip(kc - qc, -(NA_COLS - 1), NA_COLS - 1) + NA_COLS - 1
    cols = jnp.where(jnp.asarray(in_win), rpb[:, :, col_idx].astype(F32) * LOG2E, -jnp.inf)
    outside = jnp.full((N_HEADS_B, GRID_W, GRID_W), -jnp.inf, F32)
    q_pos = (0, NA_MAX_ROWS // 2, NA_MAX_ROWS)
    patterns = []
    for p in range(3):
        key_rows = []
        for kr in range(NA_BAND_ROWS):
            blocks = []
            for qi in range(NA_GROUP_ROWS):
                w_start = (0, qi, NA_MAX_ROWS // 2)[p]
                inside = w_start <= kr < w_start + NA_MAX_ROWS
                blocks.append(cols[:, kr - (q_pos[p] + qi) + NA_MAX_ROWS - 1] if inside else outside)
            key_rows.append(jnp.concatenate(blocks, axis=-1))
        patterns.append(jnp.concatenate(key_rows, axis=1))
    tab = jnp.stack(patterns, axis=1)
    return tab.reshape(N_HEADS_B * 3, NA_BAND_ROWS * GRID_W, NA_GROUP_ROWS * GRID_W)


def _na_attention(qbt, kb, vbt, bias):
    B, S, W = kb.shape
    rows = S // GRID_W
    assert rows >= NA_BAND_ROWS and rows % NA_MAX_ROWS == 0
    rb = min(NA_BLOCK_ROWS, rows)
    halo = NA_MAX_ROWS * GRID_W
    per = rb // NA_MAX_ROWS
    last = rows // NA_MAX_ROWS - 1
    prev_idx = lambda i: jnp.maximum(i * per - 1, 0)
    next_idx = lambda i: jnp.minimum((i + 1) * per, last)
    k_main = pl.BlockSpec((None, rb * GRID_W, W), lambda b, i: (b, i, 0))
    k_prev = pl.BlockSpec((None, halo, W), lambda b, i: (b, prev_idx(i), 0))
    k_next = pl.BlockSpec((None, halo, W), lambda b, i: (b, next_idx(i), 0))
    q_main = pl.BlockSpec((None, W, rb * GRID_W), lambda b, i: (b, 0, i))
    vw = N_HEADS_B * V_ROWS
    v_main = pl.BlockSpec((None, vw, rb * GRID_W), lambda b, i: (b, 0, i))
    v_prev = pl.BlockSpec((None, vw, halo), lambda b, i: (b, 0, prev_idx(i)))
    v_next = pl.BlockSpec((None, vw, halo), lambda b, i: (b, 0, next_idx(i)))
    kern = functools.partial(_na_kernel, rows=rows, rb=rb)
    band_tokens = (rb + 2 * NA_MAX_ROWS) * GRID_W
    return pl.pallas_call(
        kern,
        grid=(B, rows // rb),
        in_specs=[q_main, k_prev, k_main, k_next, v_prev, v_main, v_next, _const_spec(bias.shape)],
        out_specs=k_main,
        out_shape=jax.ShapeDtypeStruct((B, S, W), BF16),
        scratch_shapes=[pltpu.VMEM((band_tokens, W), BF16), pltpu.VMEM((vw, band_tokens), BF16),
                        pltpu.VMEM((SCORE_SLOTS, NA_BAND_ROWS * GRID_W, NA_GROUP_ROWS * GRID_W), F32)],
        compiler_params=_params(2, 56),
        name="natten",
    )(qbt, kb, kb, kb, vbt, vbt, vbt, bias)


def _merge_kernel(x_ref, oa_ref, ob_ref, nw_ref, wg_ref, bg_ref, wpa_ref, wpb_ref, wout_ref, pmn_ref, o_ref):
    x = x_ref[...]
    h = _rms(x, nw_ref[...]).astype(BF16)
    g = jnp.dot(h, wg_ref[...], preferred_element_type=F32) + bg_ref[...]
    gate = jax.nn.sigmoid(g)
    pa = jnp.dot(oa_ref[...], wpa_ref[...], preferred_element_type=F32)
    pb = jnp.dot(ob_ref[...], wpb_ref[...], preferred_element_type=F32)
    mix = gate[:, :D_MODEL] * pa + gate[:, D_MODEL:] * pb
    mo = jnp.dot(mix.astype(BF16), wout_ref[...], preferred_element_type=F32)
    o_ref[...] = x + _rms(mo, pmn_ref[...])


def _merge(x2, oa2, ob2, nw, wg, bg, wpa, wpb, wout, pmn):
    n, D = x2.shape
    tm = min(PROJ_TILE, n)
    row = lambda w: pl.BlockSpec((tm, w), lambda i: (i, 0))
    return pl.pallas_call(
        _merge_kernel,
        grid=(n // tm,),
        in_specs=[row(D), row(WIDTH_A), row(WIDTH_B), _const_spec((1, D)), _const_spec(wg.shape),
                  _const_spec((1, 2 * D)), _const_spec(wpa.shape), _const_spec(wpb.shape),
                  _const_spec(wout.shape), _const_spec((1, D))],
        out_specs=row(D),
        out_shape=jax.ShapeDtypeStruct((n, D), F32),
        compiler_params=_params(1, 40),
        name="merge",
    )(x2, oa2, ob2, nw, wg, bg, wpa, wpb, wout, pmn)


def _gelu_tanh(x):
    inner = x * (GELU_C + (GELU_C * GELU_K) * (x * x))
    return x * (0.5 + 0.5 * jnp.tanh(inner))


def _ffn_kernel(xm_ref, xp_ref, xn_ref, nw_ref, wup_ref, cw_ref, cb_ref, wdn_ref, pfn_ref, o_ref,
                f_scr, *, tiles_per_seq):
    tm = xm_ref.shape[0]
    ext = tm + 2 * HALO
    j = pl.program_id(0) % tiles_per_seq
    nw = nw_ref[...]
    xm = xm_ref[...]
    hp = _rms(xp_ref[...], nw) * (j > 0).astype(F32)
    hn = _rms(xn_ref[...], nw) * (j < tiles_per_seq - 1).astype(F32)
    h_ext = jnp.concatenate([hp, _rms(xm, nw), hn], axis=0).astype(BF16)
    main = slice(HALO, HALO + tm)

    def up(c):
        cols = [slice(half + c * FFN_CHUNK, half + (c + 1) * FFN_CHUNK) for half in (0, D_FF)]
        return [(jnp.dot(h_ext, wup_ref[:, cs], preferred_element_type=F32), cs) for cs in cols]

    def conv(u, cols):
        before = pltpu.roll(u, 1, 0)[main]
        after = pltpu.roll(u, ext - 1, 0)[main]
        acc = before * cw_ref[0:1, cols] + cb_ref[:, cols]
        acc = acc + u[main] * cw_ref[1:2, cols]
        return acc + after * cw_ref[2:3, cols]

    def down(part):
        rows = slice(part[0] * FFN_CHUNK, (part[-1] + 1) * FFN_CHUNK)
        return jnp.dot(f_scr[:, rows], wdn_ref[rows, :], preferred_element_type=F32)

    n_chunks = D_FF // FFN_CHUNK
    parts = [[int(c) for c in p] for p in np.array_split(np.arange(n_chunks), FFN_DOWN_PARTS)]
    part_ending_at = {p[-1]: p for p in parts}
    y = None
    ready = None
    u_next = up(0)
    for c in range(n_chunks):
        (gate_u, gate_cols), (val_u, val_cols) = u_next
        if c + 1 < n_chunks:
            u_next = up(c + 1)
        if ready is not None:
            d = down(ready)
            y = d if y is None else y + d
        f = _gelu_tanh(conv(gate_u, gate_cols)) * conv(val_u, val_cols)
        f_scr[:, c * FFN_CHUNK:(c + 1) * FFN_CHUNK] = f.astype(BF16)
        ready = part_ending_at.get(c)
    o_ref[...] = xm + _rms(y + down(ready), pfn_ref[...])


def _ffn(x2, seq_len, nw, wup, cw, cb, wdn, pfn):
    n, D = x2.shape
    tm = min(FFN_TILE, seq_len)
    per = tm // HALO
    last = n // HALO - 1
    kern = functools.partial(_ffn_kernel, tiles_per_seq=seq_len // tm)
    return pl.pallas_call(
        kern,
        grid=(n // tm,),
        in_specs=[pl.BlockSpec((tm, D), lambda i: (i, 0)),
                  pl.BlockSpec((HALO, D), lambda i: (jnp.maximum(i * per - 1, 0), 0)),
                  pl.BlockSpec((HALO, D), lambda i: (jnp.minimum((i + 1) * per, last), 0)),
                  _const_spec((1, D)), _const_spec(wup.shape), _const_spec(cw.shape), _const_spec(cb.shape),
                  _const_spec(wdn.shape), _const_spec((1, D))],
        out_specs=pl.BlockSpec((tm, D), lambda i: (i, 0)),
        out_shape=jax.ShapeDtypeStruct((n, D), F32),
        scratch_shapes=[pltpu.VMEM((tm, D_FF), BF16)],
        compiler_params=_params(1, 48),
        name="ffn",
    )(x2, x2, x2, nw, wup, cw, cb, wdn, pfn)


def _rope_tables_t(seq_len):
    t = np.arange(seq_len)
    row = (t // GRID_W).astype(np.float32)
    col = (t % GRID_W).astype(np.float32)
    half = HEAD_DIM // 2
    freqs = (ROPE_THETA ** (-np.arange(0, half, 2, dtype=np.float32) / half)).astype(np.float32)
    ang_r = row[:, None] * freqs[None, :]
    ang_c = col[:, None] * freqs[None, :]
    ang = np.concatenate([ang_r, ang_r, ang_c, ang_c], axis=-1).astype(np.float32)
    return jnp.asarray(np.cos(ang).T, F32), jnp.asarray(np.sin(ang).T, F32)


def _prepare_weights(pre_mix_norm, w_in, b_gate, q_norm, k_norm, rpb, w_proj_a, w_proj_b, w_out,
                     post_mix_norm, pre_ffn_norm, w_up, conv_w, conv_b, w_down, post_ffn_norm):
    row = lambda v: v.reshape(1, -1).astype(F32)
    kb_lo = WIDTH_QKV_A + WIDTH_B
    kb_hi = kb_lo + WIDTH_B
    gate_lo = WIDTH_QKV_A + WIDTH_QKV_B
    return dict(
        nw1=row(pre_mix_norm),
        wt=jnp.concatenate([w_in[:, :kb_lo], w_in[:, kb_hi:gate_lo]], axis=1).T.astype(BF16),
        wkb=w_in[:, kb_lo:kb_hi].astype(BF16),
        wg=w_in[:, gate_lo:].astype(BF16),
        bg=row(b_gate),
        qn=q_norm.reshape(HEAD_DIM, 1).astype(F32),
        kn=k_norm.reshape(HEAD_DIM, 1).astype(F32),
        bias=_na_bias_table(rpb),
        wpa=w_proj_a.astype(BF16), wpb=w_proj_b.astype(BF16), wout=w_out.astype(BF16),
        pmn=row(post_mix_norm), nw2=row(pre_ffn_norm),
        wup=w_up.astype(BF16), cw=conv_w.astype(F32), cb=row(conv_b),
        wdn=w_down.astype(BF16), pfn=row(post_ffn_norm),
    )


def _encoder_layer(x, w):
    B, S, D = x.shape
    cos_t, sin_t = _rope_tables_t(S)
    qt, k, vt, qbt, kb, vbt = _project(x, w["nw1"], w["wt"], w["wkb"], cos_t, sin_t, w["qn"], w["kn"])
    oa = _gqa_attention(qt, k, vt)
    ob = _na_attention(qbt, kb, vbt, w["bias"])
    x2 = x.reshape(B * S, D)
    x2 = _merge(x2, oa.reshape(B * S, WIDTH_A), ob.reshape(B * S, WIDTH_B), w["nw1"], w["wg"], w["bg"],
                w["wpa"], w["wpb"], w["wout"], w["pmn"])
    x2 = _ffn(x2, S, w["nw2"], w["wup"], w["cw"], w["cb"], w["wdn"], w["pfn"])
    return x2.reshape(B, S, D)


def kernel(x_prompt, x_sample, pre_mix_norm, w_in, b_gate, q_norm, k_norm, rpb, w_proj_a, w_proj_b, w_out,
           post_mix_norm, pre_ffn_norm, w_up, conv_w, conv_b, w_down, post_ffn_norm):
    layers = [_prepare_weights(pre_mix_norm[l], w_in[l], b_gate[l], q_norm[l], k_norm[l], rpb[l],
                               w_proj_a[l], w_proj_b[l], w_out[l], post_mix_norm[l], pre_ffn_norm[l],
                               w_up[l], conv_w[l], conv_b[l], w_down[l], post_ffn_norm[l])
              for l in range(w_in.shape[0])]

    def run_trunk(x):
        for w in layers:
            x = _encoder_layer(x, w)
        return x

    return run_trunk(x_prompt), run_trunk(x_sample)
```

```python
import functools

import numpy as np
import jax
import jax.numpy as jnp
from jax import lax
from jax.experimental import pallas as pl
from jax.experimental.pallas import tpu as pltpu

D_MODEL = 1024
GRID_W = 64
HEAD_DIM = 64
N_HEADS_A = 8
N_KV_HEADS_A = 2
N_HEADS_B = 8
WIDTH_A = N_HEADS_A * HEAD_DIM
WIDTH_KV_A = N_KV_HEADS_A * HEAD_DIM
WIDTH_B = N_HEADS_B * HEAD_DIM
NA_MAX_ROWS = 8
NA_COLS = 16
ROPE_THETA = 10000.0
D_FF = 2816
CONV_WIDTH = 3
EPS = 1e-6
SCALE = HEAD_DIM ** -0.5
LOG2E = float(np.log2(np.e))
Q_SCALE = SCALE * LOG2E
GROUP_A = N_HEADS_A // N_KV_HEADS_A
WIDTH_QKV_A = WIDTH_A + 2 * WIDTH_KV_A
WIDTH_QKV_B = 3 * WIDTH_B

F32 = jnp.float32
BF16 = jnp.bfloat16
MIB = 1024 * 1024

SUBLANES = 8
LANES = 128
MXU_COUNT = 2
MXU_TILE = 256

ONES_ROWS = 16
V_ROWS = HEAD_DIM + ONES_ROWS

PROJ_TILE = 1024
FFN_TILE = 512
GQA_TQ = 512
GQA_TK = 256
GQA_UNROLL = 4
NA_BLOCK_ROWS = 32
NA_GROUP_ROWS = 4
NA_BAND_ROWS = 12
NA_UNROLL = 4
SCORE_SLOTS = 4
SCORE_AHEAD = 3
FFN_CHUNK = 256
HALO = SUBLANES
FFN_DOWN_PARTS = 4
GELU_C = float(np.sqrt(2.0 / np.pi))
GELU_K = 0.044715


def _const_spec(shape):
    n = len(shape)
    return pl.BlockSpec(shape, lambda *_: (0,) * n, pipeline_mode=pl.Buffered(1))


def _params(n_axes, vmem_mib):
    return pltpu.CompilerParams(dimension_semantics=("parallel",) * n_axes,
                                vmem_limit_bytes=vmem_mib * MIB)


def _rms(x, w):
    return x * lax.rsqrt(jnp.mean(x * x, axis=-1, keepdims=True) + EPS) * w


def _proj_kernel(x_ref, nw_ref, wt_ref, wkb_ref, cos_ref, sin_ref, qn_ref, kn_ref,
                 qt_ref, k_ref, vt_ref, qbt_ref, kb_ref, vbt_ref):
    h = _rms(x_ref[...], nw_ref[...]).astype(BF16)
    pt = lax.dot_general(wt_ref[...], h, (((1,), (1,)), ((), ())), preferred_element_type=F32)
    cos = cos_ref[...]
    sin = sin_ref[...]

    def norm_rope(t, w):
        t = t * lax.rsqrt(jnp.mean(t * t, axis=0, keepdims=True) + EPS) * w
        q4 = HEAD_DIM // 4
        rot = jnp.concatenate([-t[q4:2 * q4], t[0:q4], -t[3 * q4:4 * q4], t[2 * q4:3 * q4]], axis=0)
        return t * cos + rot * sin

    qn = qn_ref[...]
    kn = kn_ref[...]
    for hh in range(N_HEADS_A):
        lo = hh * HEAD_DIM
        qt_ref[lo:lo + HEAD_DIM, :] = (norm_rope(pt[lo:lo + HEAD_DIM], qn) * Q_SCALE).astype(BF16)
    kt = jnp.concatenate(
        [norm_rope(pt[WIDTH_A + g * HEAD_DIM:WIDTH_A + (g + 1) * HEAD_DIM], kn) for g in range(N_KV_HEADS_A)],
        axis=0)
    k_ref[...] = kt.T.astype(BF16)
    ones = jnp.ones((ONES_ROWS, pt.shape[1]), BF16)

    def put_values(ref, first_row, n_heads):
        for hh in range(n_heads):
            lo = first_row + hh * HEAD_DIM
            ref[hh * V_ROWS:hh * V_ROWS + HEAD_DIM, :] = pt[lo:lo + HEAD_DIM].astype(BF16)
            ref[hh * V_ROWS + HEAD_DIM:(hh + 1) * V_ROWS, :] = ones

    put_values(vt_ref, WIDTH_A + WIDTH_KV_A, N_KV_HEADS_A)
    qbt_ref[...] = (pt[WIDTH_QKV_A:WIDTH_QKV_A + WIDTH_B] * Q_SCALE).astype(BF16)
    put_values(vbt_ref, WIDTH_QKV_A + WIDTH_B, N_HEADS_B)
    kb_ref[...] = jnp.dot(h, wkb_ref[...], preferred_element_type=F32).astype(BF16)


def _project(x, nw, wt, wkb, cos_t, sin_t, qn, kn):
    B, S, D = x.shape
    tm = min(PROJ_TILE, S)
    grid = (B, S // tm)
    row_blk = lambda w: pl.BlockSpec((None, tm, w), lambda b, i: (b, i, 0))
    col_blk = lambda w: pl.BlockSpec((None, w, tm), lambda b, i: (b, 0, i))
    return pl.pallas_call(
        _proj_kernel,
        grid=grid,
        in_specs=[row_blk(D), _const_spec((1, D)), _const_spec(wt.shape), _const_spec(wkb.shape),
                  pl.BlockSpec((HEAD_DIM, tm), lambda b, i: (0, i)),
                  pl.BlockSpec((HEAD_DIM, tm), lambda b, i: (0, i)),
                  _const_spec((HEAD_DIM, 1)), _const_spec((HEAD_DIM, 1))],
        out_specs=[col_blk(WIDTH_A), row_blk(WIDTH_KV_A), col_blk(N_KV_HEADS_A * V_ROWS),
                   col_blk(WIDTH_B), row_blk(WIDTH_B), col_blk(N_HEADS_B * V_ROWS)],
        out_shape=[jax.ShapeDtypeStruct((B, WIDTH_A, S), BF16),
                   jax.ShapeDtypeStruct((B, S, WIDTH_KV_A), BF16),
                   jax.ShapeDtypeStruct((B, N_KV_HEADS_A * V_ROWS, S), BF16),
                   jax.ShapeDtypeStruct((B, WIDTH_B, S), BF16),
                   jax.ShapeDtypeStruct((B, S, WIDTH_B), BF16),
                   jax.ShapeDtypeStruct((B, N_HEADS_B * V_ROWS, S), BF16)],
        compiler_params=_params(2, 40),
        name="proj",
    )(x, nw, wt, wkb, cos_t, sin_t, qn, kn)


def _gqa_kernel(qt_ref, k_ref, vt_ref, o_ref, qp_ref, m_ref, acc_ref, s_scr, *, tk, n_kv):
    tq = qt_ref.shape[1]
    zeros = jnp.zeros((HEAD_DIM, tq), BF16)
    for hh in range(N_HEADS_A):
        g = hh // GROUP_A
        parts = [zeros] * N_KV_HEADS_A
        parts[g] = qt_ref[hh * HEAD_DIM:(hh + 1) * HEAD_DIM, :]
        qp_ref[hh] = jnp.concatenate(parts, axis=0)
    m_ref[...] = jnp.full(m_ref.shape, -jnp.inf, F32)
    acc_ref[...] = jnp.zeros(acc_ref.shape, F32)

    def put_scores(slot, j, hh):
        for kt in range(tk // MXU_TILE):
            rows = pl.ds(pl.multiple_of(j * tk + kt * MXU_TILE, MXU_TILE), MXU_TILE)
            for qt in range(tq // MXU_TILE):
                cols = slice(qt * MXU_TILE, (qt + 1) * MXU_TILE)
                s_scr[slot, kt * MXU_TILE:(kt + 1) * MXU_TILE, cols] = jnp.dot(
                    k_ref[rows, :], qp_ref[hh, :, cols], preferred_element_type=F32)

    for ahead in range(SCORE_AHEAD):
        put_scores(ahead, 0, ahead)

    def body(j, carry):
        start = pl.multiple_of(j * tk, tk)
        j_next = jnp.minimum(j + 1, n_kv - 1)
        for hh in range(N_HEADS_A):
            g = hh // GROUP_A
            ahead = hh + SCORE_AHEAD
            nxt = (j, ahead) if ahead < N_HEADS_A else (j_next, ahead - N_HEADS_A)
            put_scores(ahead % SCORE_SLOTS, *nxt)
            m_old = m_ref[hh]
            m_new = jnp.maximum(m_old, jnp.max(s_scr[hh % SCORE_SLOTS], axis=0, keepdims=True))
            alpha = jnp.exp2(m_old - m_new)
            pb = jnp.exp2(s_scr[hh % SCORE_SLOTS] - m_new).astype(BF16)
            m_ref[hh] = m_new
            pv_cols = []
            for qt in range(tq // MXU_TILE):
                cols = slice(qt * MXU_TILE, (qt + 1) * MXU_TILE)
                pv = None
                for kt in range(tk // MXU_TILE):
                    vc = vt_ref[g * V_ROWS:(g + 1) * V_ROWS, pl.ds(start + kt * MXU_TILE, MXU_TILE)]
                    d = jnp.dot(vc, pb[kt * MXU_TILE:(kt + 1) * MXU_TILE, cols], preferred_element_type=F32)
                    pv = d if pv is None else pv + d
                pv_cols.append(pv)
            pv = jnp.concatenate(pv_cols, axis=1)
            acc_ref[hh] = alpha * acc_ref[hh] + pv
        return carry

    lax.fori_loop(0, n_kv, body, 0, unroll=min(GQA_UNROLL, max(n_kv // 2, 1)))
    for pair in range(N_HEADS_A // 2):
        o_pair = jnp.concatenate([acc_ref[hh, :HEAD_DIM] / acc_ref[hh, HEAD_DIM:HEAD_DIM + 1]
                                  for hh in (2 * pair, 2 * pair + 1)], axis=0)
        o_ref[:, pair * LANES:(pair + 1) * LANES] = o_pair.T.astype(BF16)


def _gqa_attention(qt, k, vt):
    B, _, S = qt.shape
    tq = min(GQA_TQ, S)
    tk = min(GQA_TK, S)
    kern = functools.partial(_gqa_kernel, tk=tk, n_kv=S // tk)
    return pl.pallas_call(
        kern,
        grid=(B, S // tq),
        in_specs=[pl.BlockSpec((None, WIDTH_A, tq), lambda b, i: (b, 0, i)),
                  pl.BlockSpec((None, S, WIDTH_KV_A), lambda b, i: (b, 0, 0)),
                  pl.BlockSpec((None, N_KV_HEADS_A * V_ROWS, S), lambda b, i: (b, 0, 0))],
        out_specs=pl.BlockSpec((None, tq, WIDTH_A), lambda b, i: (b, i, 0)),
        out_shape=jax.ShapeDtypeStruct((B, S, WIDTH_A), BF16),
        scratch_shapes=[pltpu.VMEM((N_HEADS_A, WIDTH_KV_A, tq), BF16),
                        pltpu.VMEM((N_HEADS_A, 1, tq), F32),
                        pltpu.VMEM((N_HEADS_A, V_ROWS, tq), F32),
                        pltpu.VMEM((SCORE_SLOTS, tk, tq), F32)],
        compiler_params=_params(2, 40),
        name="gqa",
    )(qt, k, vt)


def _na_kernel(qt_ref, kp_ref, km_ref, kn_ref, vp_ref, vm_ref, vn_ref, bias_ref, o_ref,
               kband, vband, s_scr, *, rows, rb):
    i = pl.program_id(1)
    halo = NA_MAX_ROWS * GRID_W
    main = rb * GRID_W
    if rows == rb:
        k_src, v_src, band_row0 = km_ref, vm_ref, 0
    else:
        kband[0:halo, :] = kp_ref[...]
        kband[halo:halo + main, :] = km_ref[...]
        kband[halo + main:, :] = kn_ref[...]
        vband[:, 0:halo] = vp_ref[...]
        vband[:, halo:halo + main] = vm_ref[...]
        vband[:, halo + main:] = vn_ref[...]
        k_src, v_src, band_row0 = kband, vband, NA_MAX_ROWS
    nq = NA_GROUP_ROWS * GRID_W
    nk = NA_BAND_ROWS * GRID_W
    zeros = jnp.zeros((HEAD_DIM, nq), BF16)
    n_groups = rows // NA_GROUP_ROWS

    n_local = rb // NA_GROUP_ROWS

    def geometry(gl):
        grp = i * n_local + gl
        band_start = jnp.clip(grp * NA_GROUP_ROWS - NA_MAX_ROWS // 2, 0, rows - NA_BAND_ROWS)
        base = pl.multiple_of((band_start - i * rb + band_row0) * GRID_W, nq)
        qcol = pl.multiple_of(gl * nq, nq)
        pat = jnp.where(grp == 0, 0, jnp.where(grp == n_groups - 1, 2, 1))
        return base, qcol, pat

    def put_scores(slot, geo, hh):
        base, qcol, _ = geo
        pair, sub = divmod(hh, 2)
        qh = qt_ref[hh * HEAD_DIM:(hh + 1) * HEAD_DIM, pl.ds(qcol, nq)]
        qp = jnp.concatenate([qh, zeros] if sub == 0 else [zeros, qh], axis=0)
        half = nk // MXU_COUNT
        for part in range(MXU_COUNT):
            kb = k_src[pl.ds(base + part * half, half), pair * LANES:(pair + 1) * LANES]
            s_scr[slot, part * half:(part + 1) * half, :] = jnp.dot(kb, qp, preferred_element_type=F32)

    for ahead in range(SCORE_AHEAD):
        put_scores(ahead, geometry(0), ahead)

    def group_body(gl, carry):
        geo = geometry(gl)
        geo_next = geometry(jnp.minimum(gl + 1, n_local - 1))
        base, qcol, pat = geo
        outs = []
        for hh in range(N_HEADS_B):
            ahead = hh + SCORE_AHEAD
            nxt = (geo, ahead) if ahead < N_HEADS_B else (geo_next, ahead - N_HEADS_B)
            put_scores(ahead % SCORE_SLOTS, *nxt)
            s = s_scr[hh % SCORE_SLOTS] + bias_ref[hh * 3 + pat]
            m = jnp.max(s, axis=0, keepdims=True)
            eb = jnp.exp2(s - m).astype(BF16)
            pv = None
            for c in range(nk // MXU_TILE):
                vt = v_src[hh * V_ROWS:(hh + 1) * V_ROWS, pl.ds(base + c * MXU_TILE, MXU_TILE)]
                d = jnp.dot(vt, eb[c * MXU_TILE:(c + 1) * MXU_TILE], preferred_element_type=F32)
                pv = d if pv is None else pv + d
            outs.append(pv[:HEAD_DIM] / pv[HEAD_DIM:HEAD_DIM + 1])
            if hh % 2 == 1:
                o_pair = jnp.concatenate(outs, axis=0)
                o_ref[pl.ds(qcol, nq), (hh // 2) * LANES:(hh // 2 + 1) * LANES] = o_pair.T.astype(BF16)
                outs = []
        return carry

    lax.fori_loop(0, n_local, group_body, 0, unroll=min(NA_UNROLL, max(n_local // 2, 1)))


def _na_bias_table(rpb):
    c = np.arange(GRID_W)
    kc, qc = c[:, None], c[None, :]
    col_start = np.clip(qc - NA_COLS // 2, 0, GRID_W - NA_COLS)
    in_win = (kc >= col_start) & (kc < col_start + NA_COLS)
    col_idx = np.clip(kc - qc, -(NA_COLS - 1), NA_COLS - 1) + NA_COLS - 1
    cols = jnp.where(jnp.asarray(in_win), rpb[:, :, col_idx].astype(F32) * LOG2E, -jnp.inf)
    outside = jnp.full((N_HEADS_B, GRID_W, GRID_W), -jnp.inf, F32)
    q_pos = (0, NA_MAX_ROWS // 2, NA_MAX_ROWS)
    patterns = []
    for p in range(3):
        key_rows = []
        for kr in range(NA_BAND_ROWS):
            blocks = []
            for qi in range(NA_GROUP_ROWS):
                w_start = (0, qi, NA_MAX_ROWS // 2)[p]
                inside = w_start <= kr < w_start + NA_MAX_ROWS
                blocks.append(cols[:, kr - (q_pos[p] + qi) + NA_MAX_ROWS - 1] if inside else outside)
            key_rows.append(jnp.concatenate(blocks, axis=-1))
        patterns.append(jnp.concatenate(key_rows, axis=1))
    tab = jnp.stack(patterns, axis=1)
    return tab.reshape(N_HEADS_B * 3, NA_BAND_ROWS * GRID_W, NA_GROUP_ROWS * GRID_W)


def _na_attention(qbt, kb, vbt, bias):
    B, S, W = kb.shape
    rows = S // GRID_W
    assert rows >= NA_BAND_ROWS and rows % NA_MAX_ROWS == 0
    rb = min(NA_BLOCK_ROWS, rows)
    halo = NA_MAX_ROWS * GRID_W
    per = rb // NA_MAX_ROWS
    last = rows // NA_MAX_ROWS - 1
    prev_idx = lambda i: jnp.maximum(i * per - 1, 0)
    next_idx = lambda i: jnp.minimum((i + 1) * per, last)
    k_main = pl.BlockSpec((None, rb * GRID_W, W), lambda b, i: (b, i, 0))
    k_prev = pl.BlockSpec((None, halo, W), lambda b, i: (b, prev_idx(i), 0))
    k_next = pl.BlockSpec((None, halo, W), lambda b, i: (b, next_idx(i), 0))
    q_main = pl.BlockSpec((None, W, rb * GRID_W), lambda b, i: (b, 0, i))
    vw = N_HEADS_B * V_ROWS
    v_main = pl.BlockSpec((None, vw, rb * GRID_W), lambda b, i: (b, 0, i))
    v_prev = pl.BlockSpec((None, vw, halo), lambda b, i: (b, 0, prev_idx(i)))
    v_next = pl.BlockSpec((None, vw, halo), lambda b, i: (b, 0, next_idx(i)))
    kern = functools.partial(_na_kernel, rows=rows, rb=rb)
    band_tokens = (rb + 2 * NA_MAX_ROWS) * GRID_W
    return pl.pallas_call(
        kern,
        grid=(B, rows // rb),
        in_specs=[q_main, k_prev, k_main, k_next, v_prev, v_main, v_next, _const_spec(bias.shape)],
        out_specs=k_main,
        out_shape=jax.ShapeDtypeStruct((B, S, W), BF16),
        scratch_shapes=[pltpu.VMEM((band_tokens, W), BF16), pltpu.VMEM((vw, band_tokens), BF16),
                        pltpu.VMEM((SCORE_SLOTS, NA_BAND_ROWS * GRID_W, NA_GROUP_ROWS * GRID_W), F32)],
        compiler_params=_params(2, 56),
        name="natten",
    )(qbt, kb, kb, kb, vbt, vbt, vbt, bias)


def _merge_kernel(x_ref, oa_ref, ob_ref, nw_ref, wg_ref, bg_ref, wpa_ref, wpb_ref, wout_ref, pmn_ref, o_ref):
    x = x_ref[...]
    h = _rms(x, nw_ref[...]).astype(BF16)
    g = jnp.dot(h, wg_ref[...], preferred_element_type=F32) + bg_ref[...]
    gate = jax.nn.sigmoid(g)
    pa = jnp.dot(oa_ref[...], wpa_ref[...], preferred_element_type=F32)
    pb = jnp.dot(ob_ref[...], wpb_ref[...], preferred_element_type=F32)
    mix = gate[:, :D_MODEL] * pa + gate[:, D_MODEL:] * pb
    mo = jnp.dot(mix.astype(BF16), wout_ref[...], preferred_element_type=F32)
    o_ref[...] = x + _rms(mo, pmn_ref[...])


def _merge(x2, oa2, ob2, nw, wg, bg, wpa, wpb, wout, pmn):
    n, D = x2.shape
    tm = min(PROJ_TILE, n)
    row = lambda w: pl.BlockSpec((tm, w), lambda i: (i, 0))
    return pl.pallas_call(
        _merge_kernel,
        grid=(n // tm,),
        in_specs=[row(D), row(WIDTH_A), row(WIDTH_B), _const_spec((1, D)), _const_spec(wg.shape),
                  _const_spec((1, 2 * D)), _const_spec(wpa.shape), _const_spec(wpb.shape),
                  _const_spec(wout.shape), _const_spec((1, D))],
        out_specs=row(D),
        out_shape=jax.ShapeDtypeStruct((n, D), F32),
        compiler_params=_params(1, 40),
        name="merge",
    )(x2, oa2, ob2, nw, wg, bg, wpa, wpb, wout, pmn)


def _gelu_tanh(x):
    inner = x * (GELU_C + (GELU_C * GELU_K) * (x * x))
    return x * (0.5 + 0.5 * jnp.tanh(inner))


def _ffn_kernel(xm_ref, xp_ref, xn_ref, nw_ref, wup_ref, cw_ref, cb_ref, wdn_ref, pfn_ref, o_ref,
                f_scr, *, tiles_per_seq):
    tm = xm_ref.shape[0]
    ext = tm + 2 * HALO
    j = pl.program_id(0) % tiles_per_seq
    nw = nw_ref[...]
    xm = xm_ref[...]
    hp = _rms(xp_ref[...], nw) * (j > 0).astype(F32)
    hn = _rms(xn_ref[...], nw) * (j < tiles_per_seq - 1).astype(F32)
    h_ext = jnp.concatenate([hp, _rms(xm, nw), hn], axis=0).astype(BF16)
    main = slice(HALO, HALO + tm)

    def up(c):
        cols = [slice(half + c * FFN_CHUNK, half + (c + 1) * FFN_CHUNK) for half in (0, D_FF)]
        return [(jnp.dot(h_ext, wup_ref[:, cs], preferred_element_type=F32), cs) for cs in cols]

    def conv(u, cols):
        before = pltpu.roll(u, 1, 0)[main]
        after = pltpu.roll(u, ext - 1, 0)[main]
        acc = before * cw_ref[0:1, cols] + cb_ref[:, cols]
        acc = acc + u[main] * cw_ref[1:2, cols]
        return acc + after * cw_ref[2:3, cols]

    def down(part):
        rows = slice(part[0] * FFN_CHUNK, (part[-1] + 1) * FFN_CHUNK)
        return jnp.dot(f_scr[:, rows], wdn_ref[rows, :], preferred_element_type=F32)

    n_chunks = D_FF // FFN_CHUNK
    parts = [[int(c) for c in p] for p in np.array_split(np.arange(n_chunks), FFN_DOWN_PARTS)]
    part_ending_at = {p[-1]: p for p in parts}
    y = None
    ready = None
    u_next = up(0)
    for c in range(n_chunks):
        (gate_u, gate_cols), (val_u, val_cols) = u_next
        if c + 1 < n_chunks:
            u_next = up(c + 1)
        if ready is not None:
            d = down(ready)
            y = d if y is None else y + d
        f = _gelu_tanh(conv(gate_u, gate_cols)) * conv(val_u, val_cols)
        f_scr[:, c * FFN_CHUNK:(c + 1) * FFN_CHUNK] = f.astype(BF16)
        ready = part_ending_at.get(c)
    o_ref[...] = xm + _rms(y + down(ready), pfn_ref[...])


def _ffn(x2, seq_len, nw, wup, cw, cb, wdn, pfn):
    n, D = x2.shape
    tm = min(FFN_TILE, seq_len)
    per = tm // HALO
    last = n // HALO - 1
    kern = functools.partial(_ffn_kernel, tiles_per_seq=seq_len // tm)
    return pl.pallas_call(
        kern,
        grid=(n // tm,),
        in_specs=[pl.BlockSpec((tm, D), lambda i: (i, 0)),
                  pl.BlockSpec((HALO, D), lambda i: (jnp.maximum(i * per - 1, 0), 0)),
                  pl.BlockSpec((HALO, D), lambda i: (jnp.minimum((i + 1) * per, last), 0)),
                  _const_spec((1, D)), _const_spec(wup.shape), _const_spec(cw.shape), _const_spec(cb.shape),
                  _const_spec(wdn.shape), _const_spec((1, D))],
        out_specs=pl.BlockSpec((tm, D), lambda i: (i, 0)),
        out_shape=jax.ShapeDtypeStruct((n, D), F32),
        scratch_shapes=[pltpu.VMEM((tm, D_FF), BF16)],
        compiler_params=_params(1, 48),
        name="ffn",
    )(x2, x2, x2, nw, wup, cw, cb, wdn, pfn)


def _rope_tables_t(seq_len):
    t = np.arange(seq_len)
    row = (t // GRID_W).astype(np.float32)
    col = (t % GRID_W).astype(np.float32)
    half = HEAD_DIM // 2
    freqs = (ROPE_THETA ** (-np.arange(0, half, 2, dtype=np.float32) / half)).astype(np.float32)
    ang_r = row[:, None] * freqs[None, :]
    ang_c = col[:, None] * freqs[None, :]
    ang = np.concatenate([ang_r, ang_r, ang_c, ang_c], axis=-1).astype(np.float32)
    return jnp.asarray(np.cos(ang).T, F32), jnp.asarray(np.sin(ang).T, F32)


def _prepare_weights(pre_mix_norm, w_in, b_gate, q_norm, k_norm, rpb, w_proj_a, w_proj_b, w_out,
                     post_mix_norm, pre_ffn_norm, w_up, conv_w, conv_b, w_down, post_ffn_norm):
    row = lambda v: v.reshape(1, -1).astype(F32)
    kb_lo = WIDTH_QKV_A + WIDTH_B
    kb_hi = kb_lo + WIDTH_B
    gate_lo = WIDTH_QKV_A + WIDTH_QKV_B
    return dict(
        nw1=row(pre_mix_norm),
        wt=jnp.concatenate([w_in[:, :kb_lo], w_in[:, kb_hi:gate_lo]], axis=1).T.astype(BF16),
        wkb=w_in[:, kb_lo:kb_hi].astype(BF16),
        wg=w_in[:, gate_lo:].astype(BF16),
        bg=row(b_gate),
        qn=q_norm.reshape(HEAD_DIM, 1).astype(F32),
        kn=k_norm.reshape(HEAD_DIM, 1).astype(F32),
        bias=_na_bias_table(rpb),
        wpa=w_proj_a.astype(BF16), wpb=w_proj_b.astype(BF16), wout=w_out.astype(BF16),
        pmn=row(post_mix_norm), nw2=row(pre_ffn_norm),
        wup=w_up.astype(BF16), cw=conv_w.astype(F32), cb=row(conv_b),
        wdn=w_down.astype(BF16), pfn=row(post_ffn_norm),
    )


def _encoder_layer(x, w):
    B, S, D = x.shape
    cos_t, sin_t = _rope_tables_t(S)
    qt, k, vt, qbt, kb, vbt = _project(x, w["nw1"], w["wt"], w["wkb"], cos_t, sin_t, w["qn"], w["kn"])
    oa = _gqa_attention(qt, k, vt)
    ob = _na_attention(qbt, kb, vbt, w["bias"])
    x2 = x.reshape(B * S, D)
    x2 = _merge(x2, oa.reshape(B * S, WIDTH_A), ob.reshape(B * S, WIDTH_B), w["nw1"], w["wg"], w["bg"],
                w["wpa"], w["wpb"], w["wout"], w["pmn"])
    x2 = _ffn(x2, S, w["nw2"], w["wup"], w["cw"], w["cb"], w["wdn"], w["pfn"])
    return x2.reshape(B, S, D)


def kernel(x_prompt, x_sample, pre_mix_norm, w_in, b_gate, q_norm, k_norm, rpb, w_proj_a, w_proj_b, w_out,
           post_mix_norm, pre_ffn_norm, w_up, conv_w, conv_b, w_down, post_ffn_norm):
    layers = [_prepare_weights(pre_mix_norm[l], w_in[l], b_gate[l], q_norm[l], k_norm[l], rpb[l],
                               w_proj_a[l], w_proj_b[l], w_out[l], post_mix_norm[l], pre_ffn_norm[l],
                               w_up[l], conv_w[l], conv_b[l], w_down[l], post_ffn_norm[l])
              for l in range(w_in.shape[0])]

    def run_trunk(x):
        for w in layers:
            x = _encoder_layer(x, w)
        return x

    return run_trunk(x_prompt), run_trunk(x_sample)
```

```python
import functools

import numpy as np
import jax
import jax.numpy as jnp
from jax import lax
from jax.experimental import pallas as pl
from jax.experimental.pallas import tpu as pltpu

D_MODEL = 1024
GRID_W = 64
HEAD_DIM = 64
N_HEADS_A = 8
N_KV_HEADS_A = 2
N_HEADS_B = 8
WIDTH_A = N_HEADS_A * HEAD_DIM
WIDTH_KV_A = N_KV_HEADS_A * HEAD_DIM
WIDTH_B = N_HEADS_B * HEAD_DIM
NA_MAX_ROWS = 8
NA_COLS = 16
ROPE_THETA = 10000.0
D_FF = 2816
CONV_WIDTH = 3
EPS = 1e-6
SCALE = HEAD_DIM ** -0.5
LOG2E = float(np.log2(np.e))
Q_SCALE = SCALE * LOG2E
GROUP_A = N_HEADS_A // N_KV_HEADS_A
WIDTH_QKV_A = WIDTH_A + 2 * WIDTH_KV_A
WIDTH_QKV_B = 3 * WIDTH_B

F32 = jnp.float32
BF16 = jnp.bfloat16
MIB = 1024 * 1024

SUBLANES = 8
LANES = 128
MXU_COUNT = 2
MXU_TILE = 256

ONES_ROWS = 16
V_ROWS = HEAD_DIM + ONES_ROWS

PROJ_TILE = 1024
FFN_TILE = 512
GQA_TQ = 512
GQA_TK = 256
GQA_UNROLL = 4
NA_BLOCK_ROWS = 32
NA_GROUP_ROWS = 4
NA_BAND_ROWS = 12
NA_UNROLL = 4
SCORE_SLOTS = 4
GQA_AHEAD = 2
NA_AHEAD = 3
FFN_CHUNK = 256
HALO = SUBLANES
FFN_DOWN_PARTS = 4
GELU_C = float(np.sqrt(2.0 / np.pi))
GELU_K = 0.044715


def _const_spec(shape):
    n = len(shape)
    return pl.BlockSpec(shape, lambda *_: (0,) * n, pipeline_mode=pl.Buffered(1))


def _params(n_axes, vmem_mib):
    return pltpu.CompilerParams(dimension_semantics=("parallel",) * n_axes,
                                vmem_limit_bytes=vmem_mib * MIB)


def _rms(x, w):
    return x * lax.rsqrt(jnp.mean(x * x, axis=-1, keepdims=True) + EPS) * w


def _proj_kernel(x_ref, nw_ref, wt_ref, wkb_ref, cos_ref, sin_ref, qn_ref, kn_ref,
                 qt_ref, k_ref, vt_ref, qbt_ref, kb_ref, vbt_ref):
    h = _rms(x_ref[...], nw_ref[...]).astype(BF16)
    pt = lax.dot_general(wt_ref[...], h, (((1,), (1,)), ((), ())), preferred_element_type=F32)
    cos = cos_ref[...]
    sin = sin_ref[...]

    def norm_rope(t, w):
        t = t * lax.rsqrt(jnp.mean(t * t, axis=0, keepdims=True) + EPS) * w
        q4 = HEAD_DIM // 4
        rot = jnp.concatenate([-t[q4:2 * q4], t[0:q4], -t[3 * q4:4 * q4], t[2 * q4:3 * q4]], axis=0)
        return t * cos + rot * sin

    qn = qn_ref[...]
    kn = kn_ref[...]
    for hh in range(N_HEADS_A):
        lo = hh * HEAD_DIM
        qt_ref[lo:lo + HEAD_DIM, :] = (norm_rope(pt[lo:lo + HEAD_DIM], qn) * Q_SCALE).astype(BF16)
    kt = jnp.concatenate(
        [norm_rope(pt[WIDTH_A + g * HEAD_DIM:WIDTH_A + (g + 1) * HEAD_DIM], kn) for g in range(N_KV_HEADS_A)],
        axis=0)
    k_ref[...] = kt.T.astype(BF16)
    ones = jnp.ones((ONES_ROWS, pt.shape[1]), BF16)

    def put_values(ref, first_row, n_heads):
        for hh in range(n_heads):
            lo = first_row + hh * HEAD_DIM
            ref[hh * V_ROWS:hh * V_ROWS + HEAD_DIM, :] = pt[lo:lo + HEAD_DIM].astype(BF16)
            ref[hh * V_ROWS + HEAD_DIM:(hh + 1) * V_ROWS, :] = ones

    put_values(vt_ref, WIDTH_A + WIDTH_KV_A, N_KV_HEADS_A)
    qbt_ref[...] = (pt[WIDTH_QKV_A:WIDTH_QKV_A + WIDTH_B] * Q_SCALE).astype(BF16)
    put_values(vbt_ref, WIDTH_QKV_A + WIDTH_B, N_HEADS_B)
    kb_ref[...] = jnp.dot(h, wkb_ref[...], preferred_element_type=F32).astype(BF16)


def _project(x, nw, wt, wkb, cos_t, sin_t, qn, kn):
    B, S, D = x.shape
    tm = min(PROJ_TILE, S)
    grid = (B, S // tm)
    row_blk = lambda w: pl.BlockSpec((None, tm, w), lambda b, i: (b, i, 0))
    col_blk = lambda w: pl.BlockSpec((None, w, tm), lambda b, i: (b, 0, i))
    return pl.pallas_call(
        _proj_kernel,
        grid=grid,
        in_specs=[row_blk(D), _const_spec((1, D)), _const_spec(wt.shape), _const_spec(wkb.shape),
                  pl.BlockSpec((HEAD_DIM, tm), lambda b, i: (0, i)),
                  pl.BlockSpec((HEAD_DIM, tm), lambda b, i: (0, i)),
                  _const_spec((HEAD_DIM, 1)), _const_spec((HEAD_DIM, 1))],
        out_specs=[col_blk(WIDTH_A), row_blk(WIDTH_KV_A), col_blk(N_KV_HEADS_A * V_ROWS),
                   col_blk(WIDTH_B), row_blk(WIDTH_B), col_blk(N_HEADS_B * V_ROWS)],
        out_shape=[jax.ShapeDtypeStruct((B, WIDTH_A, S), BF16),
                   jax.ShapeDtypeStruct((B, S, WIDTH_KV_A), BF16),
                   jax.ShapeDtypeStruct((B, N_KV_HEADS_A * V_ROWS, S), BF16),
                   jax.ShapeDtypeStruct((B, WIDTH_B, S), BF16),
                   jax.ShapeDtypeStruct((B, S, WIDTH_B), BF16),
                   jax.ShapeDtypeStruct((B, N_HEADS_B * V_ROWS, S), BF16)],
        compiler_params=_params(2, 40),
        name="proj",
    )(x, nw, wt, wkb, cos_t, sin_t, qn, kn)


def _gqa_kernel(qt_ref, k_ref, vt_ref, o_ref, qp_ref, m_ref, acc_ref, s_scr, *, tk, n_kv):
    tq = qt_ref.shape[1]
    zeros = jnp.zeros((HEAD_DIM, tq), BF16)
    for hh in range(N_HEADS_A):
        g = hh // GROUP_A
        parts = [zeros] * N_KV_HEADS_A
        parts[g] = qt_ref[hh * HEAD_DIM:(hh + 1) * HEAD_DIM, :]
        qp_ref[hh] = jnp.concatenate(parts, axis=0)
    m_ref[...] = jnp.full(m_ref.shape, -jnp.inf, F32)
    acc_ref[...] = jnp.zeros(acc_ref.shape, F32)

    def put_scores(slot, j, hh):
        for kt in range(tk // MXU_TILE):
            rows = pl.ds(pl.multiple_of(j * tk + kt * MXU_TILE, MXU_TILE), MXU_TILE)
            for qt in range(tq // MXU_TILE):
                cols = slice(qt * MXU_TILE, (qt + 1) * MXU_TILE)
                s_scr[slot, kt * MXU_TILE:(kt + 1) * MXU_TILE, cols] = jnp.dot(
                    k_ref[rows, :], qp_ref[hh, :, cols], preferred_element_type=F32)

    for ahead in range(GQA_AHEAD):
        put_scores(ahead, 0, ahead)

    def body(j, carry):
        start = pl.multiple_of(j * tk, tk)
        j_next = jnp.minimum(j + 1, n_kv - 1)
        for hh in range(N_HEADS_A):
            g = hh // GROUP_A
            ahead = hh + GQA_AHEAD
            nxt = (j, ahead) if ahead < N_HEADS_A else (j_next, ahead - N_HEADS_A)
            put_scores(ahead % SCORE_SLOTS, *nxt)
            m_old = m_ref[hh]
            m_new = jnp.maximum(m_old, jnp.max(s_scr[hh % SCORE_SLOTS], axis=0, keepdims=True))
            alpha = jnp.exp2(m_old - m_new)
            pb = jnp.exp2(s_scr[hh % SCORE_SLOTS] - m_new).astype(BF16)
            m_ref[hh] = m_new
            pv_cols = []
            for qt in range(tq // MXU_TILE):
                cols = slice(qt * MXU_TILE, (qt + 1) * MXU_TILE)
                pv = None
                for kt in range(tk // MXU_TILE):
                    vc = vt_ref[g * V_ROWS:(g + 1) * V_ROWS, pl.ds(start + kt * MXU_TILE, MXU_TILE)]
                    d = jnp.dot(vc, pb[kt * MXU_TILE:(kt + 1) * MXU_TILE, cols], preferred_element_type=F32)
                    pv = d if pv is None else pv + d
                pv_cols.append(pv)
            pv = jnp.concatenate(pv_cols, axis=1)
            acc_ref[hh] = alpha * acc_ref[hh] + pv
        return carry

    lax.fori_loop(0, n_kv, body, 0, unroll=min(GQA_UNROLL, max(n_kv // 2, 1)))
    for pair in range(N_HEADS_A // 2):
        o_pair = jnp.concatenate([acc_ref[hh, :HEAD_DIM] / acc_ref[hh, HEAD_DIM:HEAD_DIM + 1]
                                  for hh in (2 * pair, 2 * pair + 1)], axis=0)
        o_ref[:, pair * LANES:(pair + 1) * LANES] = o_pair.T.astype(BF16)


def _gqa_attention(qt, k, vt):
    B, _, S = qt.shape
    tq = min(GQA_TQ, S)
    tk = min(GQA_TK, S)
    kern = functools.partial(_gqa_kernel, tk=tk, n_kv=S // tk)
    return pl.pallas_call(
        kern,
        grid=(B, S // tq),
        in_specs=[pl.BlockSpec((None, WIDTH_A, tq), lambda b, i: (b, 0, i)),
                  pl.BlockSpec((None, S, WIDTH_KV_A), lambda b, i: (b, 0, 0)),
                  pl.BlockSpec((None, N_KV_HEADS_A * V_ROWS, S), lambda b, i: (b, 0, 0))],
        out_specs=pl.BlockSpec((None, tq, WIDTH_A), lambda b, i: (b, i, 0)),
        out_shape=jax.ShapeDtypeStruct((B, S, WIDTH_A), BF16),
        scratch_shapes=[pltpu.VMEM((N_HEADS_A, WIDTH_KV_A, tq), BF16),
                        pltpu.VMEM((N_HEADS_A, 1, tq), F32),
                        pltpu.VMEM((N_HEADS_A, V_ROWS, tq), F32),
                        pltpu.VMEM((SCORE_SLOTS, tk, tq), F32)],
        compiler_params=_params(2, 40),
        name="gqa",
    )(qt, k, vt)


def _na_kernel(qt_ref, kp_ref, km_ref, kn_ref, vp_ref, vm_ref, vn_ref, bias_ref, o_ref,
               kband, vband, s_scr, *, rows, rb):
    i = pl.program_id(1)
    halo = NA_MAX_ROWS * GRID_W
    main = rb * GRID_W
    if rows == rb:
        k_src, v_src, band_row0 = km_ref, vm_ref, 0
    else:
        kband[0:halo, :] = kp_ref[...]
        kband[halo:halo + main, :] = km_ref[...]
        kband[halo + main:, :] = kn_ref[...]
        vband[:, 0:halo] = vp_ref[...]
        vband[:, halo:halo + main] = vm_ref[...]
        vband[:, halo + main:] = vn_ref[...]
        k_src, v_src, band_row0 = kband, vband, NA_MAX_ROWS
    nq = NA_GROUP_ROWS * GRID_W
    nk = NA_BAND_ROWS * GRID_W
    zeros = jnp.zeros((HEAD_DIM, nq), BF16)
    n_groups = rows // NA_GROUP_ROWS

    n_local = rb // NA_GROUP_ROWS

    def geometry(gl):
        grp = i * n_local + gl
        band_start = jnp.clip(grp * NA_GROUP_ROWS - NA_MAX_ROWS // 2, 0, rows - NA_BAND_ROWS)
        base = pl.multiple_of((band_start - i * rb + band_row0) * GRID_W, nq)
        qcol = pl.multiple_of(gl * nq, nq)
        pat = jnp.where(grp == 0, 0, jnp.where(grp == n_groups - 1, 2, 1))
        return base, qcol, pat

    def put_scores(slot, geo, hh):
        base, qcol, _ = geo
        pair, sub = divmod(hh, 2)
        qh = qt_ref[hh * HEAD_DIM:(hh + 1) * HEAD_DIM, pl.ds(qcol, nq)]
        qp = jnp.concatenate([qh, zeros] if sub == 0 else [zeros, qh], axis=0)
        half = nk // MXU_COUNT
        for part in range(MXU_COUNT):
            kb = k_src[pl.ds(base + part * half, half), pair * LANES:(pair + 1) * LANES]
            s_scr[slot, part * half:(part + 1) * half, :] = jnp.dot(kb, qp, preferred_element_type=F32)

    for ahead in range(NA_AHEAD):
        put_scores(ahead, geometry(0), ahead)

    def group_body(gl, carry):
        geo = geometry(gl)
        geo_next = geometry(jnp.minimum(gl + 1, n_local - 1))
        base, qcol, pat = geo
        outs = []
        for hh in range(N_HEADS_B):
            ahead = hh + NA_AHEAD
            nxt = (geo, ahead) if ahead < N_HEADS_B else (geo_next, ahead - N_HEADS_B)
            put_scores(ahead % SCORE_SLOTS, *nxt)
            s = s_scr[hh % SCORE_SLOTS] + bias_ref[hh * 3 + pat]
            m = jnp.max(s, axis=0, keepdims=True)
            eb = jnp.exp2(s - m).astype(BF16)
            pv = None
            for c in range(nk // MXU_TILE):
                vt = v_src[hh * V_ROWS:(hh + 1) * V_ROWS, pl.ds(base + c * MXU_TILE, MXU_TILE)]
                d = jnp.dot(vt, eb[c * MXU_TILE:(c + 1) * MXU_TILE], preferred_element_type=F32)
                pv = d if pv is None else pv + d
            outs.append(pv[:HEAD_DIM] / pv[HEAD_DIM:HEAD_DIM + 1])
            if hh % 2 == 1:
                o_pair = jnp.concatenate(outs, axis=0)
                o_ref[pl.ds(qcol, nq), (hh // 2) * LANES:(hh // 2 + 1) * LANES] = o_pair.T.astype(BF16)
                outs = []
        return carry

    lax.fori_loop(0, n_local, group_body, 0, unroll=min(NA_UNROLL, max(n_local // 2, 1)))


def _na_bias_table(rpb):
    c = np.arange(GRID_W)
    kc, qc = c[:, None], c[None, :]
    col_start = np.clip(qc - NA_COLS // 2, 0, GRID_W - NA_COLS)
    in_win = (kc >= col_start) & (kc < col_start + NA_COLS)
    col_idx = np.clip(kc - qc, -(NA_COLS - 1), NA_COLS - 1) + NA_COLS - 1
    cols = jnp.where(jnp.asarray(in_win), rpb[:, :, col_idx].astype(F32) * LOG2E, -jnp.inf)
    outside = jnp.full((N_HEADS_B, GRID_W, GRID_W), -jnp.inf, F32)
    q_pos = (0, NA_MAX_ROWS // 2, NA_MAX_ROWS)
    patterns = []
    for p in range(3):
        key_rows = []
        for kr in range(NA_BAND_ROWS):
            blocks = []
            for qi in range(NA_GROUP_ROWS):
                w_start = (0, qi, NA_MAX_ROWS // 2)[p]
                inside = w_start <= kr < w_start + NA_MAX_ROWS
                blocks.append(cols[:, kr - (q_pos[p] + qi) + NA_MAX_ROWS - 1] if inside else outside)
            key_rows.append(jnp.concatenate(blocks, axis=-1))
        patterns.append(jnp.concatenate(key_rows, axis=1))
    tab = jnp.stack(patterns, axis=1)
    return tab.reshape(N_HEADS_B * 3, NA_BAND_ROWS * GRID_W, NA_GROUP_ROWS * GRID_W)


def _na_attention(qbt, kb, vbt, bias):
    B, S, W = kb.shape
    rows = S // GRID_W
    assert rows >= NA_BAND_ROWS and rows % NA_MAX_ROWS == 0
    rb = min(NA_BLOCK_ROWS, rows)
    halo = NA_MAX_ROWS * GRID_W
    per = rb // NA_MAX_ROWS
    last = rows // NA_MAX_ROWS - 1
    prev_idx = lambda i: jnp.maximum(i * per - 1, 0)
    next_idx = lambda i: jnp.minimum((i + 1) * per, last)
    k_main = pl.BlockSpec((None, rb * GRID_W, W), lambda b, i: (b, i, 0))
    k_prev = pl.BlockSpec((None, halo, W), lambda b, i: (b, prev_idx(i), 0))
    k_next = pl.BlockSpec((None, halo, W), lambda b, i: (b, next_idx(i), 0))
    q_main = pl.BlockSpec((None, W, rb * GRID_W), lambda b, i: (b, 0, i))
    vw = N_HEADS_B * V_ROWS
    v_main = pl.BlockSpec((None, vw, rb * GRID_W), lambda b, i: (b, 0, i))
    v_prev = pl.BlockSpec((None, vw, halo), lambda b, i: (b, 0, prev_idx(i)))
    v_next = pl.BlockSpec((None, vw, halo), lambda b, i: (b, 0, next_idx(i)))
    kern = functools.partial(_na_kernel, rows=rows, rb=rb)
    band_tokens = (rb + 2 * NA_MAX_ROWS) * GRID_W
    return pl.pallas_call(
        kern,
        grid=(B, rows // rb),
        in_specs=[q_main, k_prev, k_main, k_next, v_prev, v_main, v_next, _const_spec(bias.shape)],
        out_specs=k_main,
        out_shape=jax.ShapeDtypeStruct((B, S, W), BF16),
        scratch_shapes=[pltpu.VMEM((band_tokens, W), BF16), pltpu.VMEM((vw, band_tokens), BF16),
                        pltpu.VMEM((SCORE_SLOTS, NA_BAND_ROWS * GRID_W, NA_GROUP_ROWS * GRID_W), F32)],
        compiler_params=_params(2, 56),
        name="natten",
    )(qbt, kb, kb, kb, vbt, vbt, vbt, bias)


def _merge_kernel(x_ref, oa_ref, ob_ref, nw_ref, wg_ref, bg_ref, wpa_ref, wpb_ref, wout_ref, pmn_ref, o_ref):
    x = x_ref[...]
    h = _rms(x, nw_ref[...]).astype(BF16)
    g = jnp.dot(h, wg_ref[...], preferred_element_type=F32) + bg_ref[...]
    gate = jax.nn.sigmoid(g)
    pa = jnp.dot(oa_ref[...], wpa_ref[...], preferred_element_type=F32)
    pb = jnp.dot(ob_ref[...], wpb_ref[...], preferred_element_type=F32)
    mix = gate[:, :D_MODEL] * pa + gate[:, D_MODEL:] * pb
    mo = jnp.dot(mix.astype(BF16), wout_ref[...], preferred_element_type=F32)
    o_ref[...] = x + _rms(mo, pmn_ref[...])


def _merge(x2, oa2, ob2, nw, wg, bg, wpa, wpb, wout, pmn):
    n, D = x2.shape
    tm = min(PROJ_TILE, n)
    row = lambda w: pl.BlockSpec((tm, w), lambda i: (i, 0))
    return pl.pallas_call(
        _merge_kernel,
        grid=(n // tm,),
        in_specs=[row(D), row(WIDTH_A), row(WIDTH_B), _const_spec((1, D)), _const_spec(wg.shape),
                  _const_spec((1, 2 * D)), _const_spec(wpa.shape), _const_spec(wpb.shape),
                  _const_spec(wout.shape), _const_spec((1, D))],
        out_specs=row(D),
        out_shape=jax.ShapeDtypeStruct((n, D), F32),
        compiler_params=_params(1, 40),
        name="merge",
    )(x2, oa2, ob2, nw, wg, bg, wpa, wpb, wout, pmn)


def _gelu_tanh(x):
    inner = x * (GELU_C + (GELU_C * GELU_K) * (x * x))
    return x * (0.5 + 0.5 * jnp.tanh(inner))


def _ffn_kernel(xm_ref, xp_ref, xn_ref, nw_ref, wup_ref, cw_ref, cb_ref, wdn_ref, pfn_ref, o_ref,
                f_scr, *, tiles_per_seq):
    tm = xm_ref.shape[0]
    ext = tm + 2 * HALO
    j = pl.program_id(0) % tiles_per_seq
    nw = nw_ref[...]
    xm = xm_ref[...]
    hp = _rms(xp_ref[...], nw) * (j > 0).astype(F32)
    hn = _rms(xn_ref[...], nw) * (j < tiles_per_seq - 1).astype(F32)
    h_ext = jnp.concatenate([hp, _rms(xm, nw), hn], axis=0).astype(BF16)
    main = slice(HALO, HALO + tm)

    def up(c):
        cols = [slice(half + c * FFN_CHUNK, half + (c + 1) * FFN_CHUNK) for half in (0, D_FF)]
        return [(jnp.dot(h_ext, wup_ref[:, cs], preferred_element_type=F32), cs) for cs in cols]

    def conv(u, cols):
        before = pltpu.roll(u, 1, 0)[main]
        after = pltpu.roll(u, ext - 1, 0)[main]
        acc = before * cw_ref[0:1, cols] + cb_ref[:, cols]
        acc = acc + u[main] * cw_ref[1:2, cols]
        return acc + after * cw_ref[2:3, cols]

    def down(part):
        rows = slice(part[0] * FFN_CHUNK, (part[-1] + 1) * FFN_CHUNK)
        return jnp.dot(f_scr[:, rows], wdn_ref[rows, :], preferred_element_type=F32)

    n_chunks = D_FF // FFN_CHUNK
    parts = [[int(c) for c in p] for p in np.array_split(np.arange(n_chunks), FFN_DOWN_PARTS)]
    part_ending_at = {p[-1]: p for p in parts}
    y = None
    ready = None
    u_next = up(0)
    for c in range(n_chunks):
        (gate_u, gate_cols), (val_u, val_cols) = u_next
        if c + 1 < n_chunks:
            u_next = up(c + 1)
        if ready is not None:
            d = down(ready)
            y = d if y is None else y + d
        f = _gelu_tanh(conv(gate_u, gate_cols)) * conv(val_u, val_cols)
        f_scr[:, c * FFN_CHUNK:(c + 1) * FFN_CHUNK] = f.astype(BF16)
        ready = part_ending_at.get(c)
    o_ref[...] = xm + _rms(y + down(ready), pfn_ref[...])


def _ffn(x2, seq_len, nw, wup, cw, cb, wdn, pfn):
    n, D = x2.shape
    tm = min(FFN_TILE, seq_len)
    per = tm // HALO
    last = n // HALO - 1
    kern = functools.partial(_ffn_kernel, tiles_per_seq=seq_len // tm)
    return pl.pallas_call(
        kern,
        grid=(n // tm,),
        in_specs=[pl.BlockSpec((tm, D), lambda i: (i, 0)),
                  pl.BlockSpec((HALO, D), lambda i: (jnp.maximum(i * per - 1, 0), 0)),
                  pl.BlockSpec((HALO, D), lambda i: (jnp.minimum((i + 1) * per, last), 0)),
                  _const_spec((1, D)), _const_spec(wup.shape), _const_spec(cw.shape), _const_spec(cb.shape),
                  _const_spec(wdn.shape), _const_spec((1, D))],
        out_specs=pl.BlockSpec((tm, D), lambda i: (i, 0)),
        out_shape=jax.ShapeDtypeStruct((n, D), F32),
        scratch_shapes=[pltpu.VMEM((tm, D_FF), BF16)],
        compiler_params=_params(1, 48),
        name="ffn",
    )(x2, x2, x2, nw, wup, cw, cb, wdn, pfn)


def _rope_tables_t(seq_len):
    t = np.arange(seq_len)
    row = (t // GRID_W).astype(np.float32)
    col = (t % GRID_W).astype(np.float32)
    half = HEAD_DIM // 2
    freqs = (ROPE_THETA ** (-np.arange(0, half, 2, dtype=np.float32) / half)).astype(np.float32)
    ang_r = row[:, None] * freqs[None, :]
    ang_c = col[:, None] * freqs[None, :]
    ang = np.concatenate([ang_r, ang_r, ang_c, ang_c], axis=-1).astype(np.float32)
    return jnp.asarray(np.cos(ang).T, F32), jnp.asarray(np.sin(ang).T, F32)


def _prepare_weights(pre_mix_norm, w_in, b_gate, q_norm, k_norm, rpb, w_proj_a, w_proj_b, w_out,
                     post_mix_norm, pre_ffn_norm, w_up, conv_w, conv_b, w_down, post_ffn_norm):
    row = lambda v: v.reshape(1, -1).astype(F32)
    kb_lo = WIDTH_QKV_A + WIDTH_B
    kb_hi = kb_lo + WIDTH_B
    gate_lo = WIDTH_QKV_A + WIDTH_QKV_B
    return dict(
        nw1=row(pre_mix_norm),
        wt=jnp.concatenate([w_in[:, :kb_lo], w_in[:, kb_hi:gate_lo]], axis=1).T.astype(BF16),
        wkb=w_in[:, kb_lo:kb_hi].astype(BF16),
        wg=w_in[:, gate_lo:].astype(BF16),
        bg=row(b_gate),
        qn=q_norm.reshape(HEAD_DIM, 1).astype(F32),
        kn=k_norm.reshape(HEAD_DIM, 1).astype(F32),
        bias=_na_bias_table(rpb),
        wpa=w_proj_a.astype(BF16), wpb=w_proj_b.astype(BF16), wout=w_out.astype(BF16),
        pmn=row(post_mix_norm), nw2=row(pre_ffn_norm),
        wup=w_up.astype(BF16), cw=conv_w.astype(F32), cb=row(conv_b),
        wdn=w_down.astype(BF16), pfn=row(post_ffn_norm),
    )


def _encoder_layer(x, w):
    B, S, D = x.shape
    cos_t, sin_t = _rope_tables_t(S)
    qt, k, vt, qbt, kb, vbt = _project(x, w["nw1"], w["wt"], w["wkb"], cos_t, sin_t, w["qn"], w["kn"])
    oa = _gqa_attention(qt, k, vt)
    ob = _na_attention(qbt, kb, vbt, w["bias"])
    x2 = x.reshape(B * S, D)
    x2 = _merge(x2, oa.reshape(B * S, WIDTH_A), ob.reshape(B * S, WIDTH_B), w["nw1"], w["wg"], w["bg"],
                w["wpa"], w["wpb"], w["wout"], w["pmn"])
    x2 = _ffn(x2, S, w["nw2"], w["wup"], w["cw"], w["cb"], w["wdn"], w["pfn"])
    return x2.reshape(B, S, D)


def kernel(x_prompt, x_sample, pre_mix_norm, w_in, b_gate, q_norm, k_norm, rpb, w_proj_a, w_proj_b, w_out,
           post_mix_norm, pre_ffn_norm, w_up, conv_w, conv_b, w_down, post_ffn_norm):
    layers = [_prepare_weights(pre_mix_norm[l], w_in[l], b_gate[l], q_norm[l], k_norm[l], rpb[l],
                               w_proj_a[l], w_proj_b[l], w_out[l], post_mix_norm[l], pre_ffn_norm[l],
                               w_up[l], conv_w[l], conv_b[l], w_down[l], post_ffn_norm[l])
              for l in range(w_in.shape[0])]

    def run_trunk(x):
        for w in layers:
            x = _encoder_layer(x, w)
        return x

    return run_trunk(x_prompt), run_trunk(x_sample)
```

```python
import functools

import numpy as np
import jax
import jax.numpy as jnp
from jax import lax
from jax.experimental import pallas as pl
from jax.experimental.pallas import tpu as pltpu

D_MODEL = 1024
GRID_W = 64
HEAD_DIM = 64
N_HEADS_A = 8
N_KV_HEADS_A = 2
N_HEADS_B = 8
WIDTH_A = N_HEADS_A * HEAD_DIM
WIDTH_KV_A = N_KV_HEADS_A * HEAD_DIM
WIDTH_B = N_HEADS_B * HEAD_DIM
NA_MAX_ROWS = 8
NA_COLS = 16
ROPE_THETA = 10000.0
D_FF = 2816
CONV_WIDTH = 3
EPS = 1e-6
SCALE = HEAD_DIM ** -0.5
LOG2E = float(np.log2(np.e))
Q_SCALE = SCALE * LOG2E
GROUP_A = N_HEADS_A // N_KV_HEADS_A
WIDTH_QKV_A = WIDTH_A + 2 * WIDTH_KV_A
WIDTH_QKV_B = 3 * WIDTH_B

F32 = jnp.float32
BF16 = jnp.bfloat16
MIB = 1024 * 1024

SUBLANES = 8
LANES = 128
MXU_COUNT = 2
MXU_TILE = 256

ONES_ROWS = 16
V_ROWS = HEAD_DIM + ONES_ROWS

PROJ_TILE = 1024
FFN_TILE = 512
GQA_TQ = 512
GQA_TK = 256
GQA_UNROLL = 4
NA_BLOCK_ROWS = 32
NA_GROUP_ROWS = 4
NA_BAND_ROWS = 12
NA_UNROLL = 4
SCORE_SLOTS = 4
SCORE_AHEAD = 3
GQA_SLOTS = 8
GQA_AHEAD = 4
FFN_CHUNK = 256
HALO = SUBLANES
FFN_DOWN_PARTS = 4
GELU_C = float(np.sqrt(2.0 / np.pi))
GELU_K = 0.044715


def _const_spec(shape):
    n = len(shape)
    return pl.BlockSpec(shape, lambda *_: (0,) * n, pipeline_mode=pl.Buffered(1))


def _params(n_axes, vmem_mib):
    return pltpu.CompilerParams(dimension_semantics=("parallel",) * n_axes,
                                vmem_limit_bytes=vmem_mib * MIB)


def _rms(x, w):
    return x * lax.rsqrt(jnp.mean(x * x, axis=-1, keepdims=True) + EPS) * w


def _proj_kernel(x_ref, nw_ref, wt_ref, wkb_ref, cos_ref, sin_ref, qn_ref, kn_ref,
                 qt_ref, k_ref, vt_ref, qbt_ref, kb_ref, vbt_ref):
    h = _rms(x_ref[...], nw_ref[...]).astype(BF16)
    pt = lax.dot_general(wt_ref[...], h, (((1,), (1,)), ((), ())), preferred_element_type=F32)
    cos = cos_ref[...]
    sin = sin_ref[...]

    def norm_rope(t, w):
        t = t * lax.rsqrt(jnp.mean(t * t, axis=0, keepdims=True) + EPS) * w
        q4 = HEAD_DIM // 4
        rot = jnp.concatenate([-t[q4:2 * q4], t[0:q4], -t[3 * q4:4 * q4], t[2 * q4:3 * q4]], axis=0)
        return t * cos + rot * sin

    qn = qn_ref[...]
    kn = kn_ref[...]
    for hh in range(N_HEADS_A):
        lo = hh * HEAD_DIM
        qt_ref[lo:lo + HEAD_DIM, :] = (norm_rope(pt[lo:lo + HEAD_DIM], qn) * Q_SCALE).astype(BF16)
    kt = jnp.concatenate(
        [norm_rope(pt[WIDTH_A + g * HEAD_DIM:WIDTH_A + (g + 1) * HEAD_DIM], kn) for g in range(N_KV_HEADS_A)],
        axis=0)
    k_ref[...] = kt.T.astype(BF16)
    ones = jnp.ones((ONES_ROWS, pt.shape[1]), BF16)

    def put_values(ref, first_row, n_heads):
        for hh in range(n_heads):
            lo = first_row + hh * HEAD_DIM
            ref[hh * V_ROWS:hh * V_ROWS + HEAD_DIM, :] = pt[lo:lo + HEAD_DIM].astype(BF16)
            ref[hh * V_ROWS + HEAD_DIM:(hh + 1) * V_ROWS, :] = ones

    put_values(vt_ref, WIDTH_A + WIDTH_KV_A, N_KV_HEADS_A)
    qbt_ref[...] = (pt[WIDTH_QKV_A:WIDTH_QKV_A + WIDTH_B] * Q_SCALE).astype(BF16)
    put_values(vbt_ref, WIDTH_QKV_A + WIDTH_B, N_HEADS_B)
    kb_ref[...] = jnp.dot(h, wkb_ref[...], preferred_element_type=F32).astype(BF16)


def _project(x, nw, wt, wkb, cos_t, sin_t, qn, kn):
    B, S, D = x.shape
    tm = min(PROJ_TILE, S)
    grid = (B, S // tm)
    row_blk = lambda w: pl.BlockSpec((None, tm, w), lambda b, i: (b, i, 0))
    col_blk = lambda w: pl.BlockSpec((None, w, tm), lambda b, i: (b, 0, i))
    return pl.pallas_call(
        _proj_kernel,
        grid=grid,
        in_specs=[row_blk(D), _const_spec((1, D)), _const_spec(wt.shape), _const_spec(wkb.shape),
                  pl.BlockSpec((HEAD_DIM, tm), lambda b, i: (0, i)),
                  pl.BlockSpec((HEAD_DIM, tm), lambda b, i: (0, i)),
                  _const_spec((HEAD_DIM, 1)), _const_spec((HEAD_DIM, 1))],
        out_specs=[col_blk(WIDTH_A), row_blk(WIDTH_KV_A), col_blk(N_KV_HEADS_A * V_ROWS),
                   col_blk(WIDTH_B), row_blk(WIDTH_B), col_blk(N_HEADS_B * V_ROWS)],
        out_shape=[jax.ShapeDtypeStruct((B, WIDTH_A, S), BF16),
                   jax.ShapeDtypeStruct((B, S, WIDTH_KV_A), BF16),
                   jax.ShapeDtypeStruct((B, N_KV_HEADS_A * V_ROWS, S), BF16),
                   jax.ShapeDtypeStruct((B, WIDTH_B, S), BF16),
                   jax.ShapeDtypeStruct((B, S, WIDTH_B), BF16),
                   jax.ShapeDtypeStruct((B, N_HEADS_B * V_ROWS, S), BF16)],
        compiler_params=_params(2, 40),
        name="proj",
    )(x, nw, wt, wkb, cos_t, sin_t, qn, kn)


def _gqa_kernel(qt_ref, k_ref, vt_ref, o_ref, qp_ref, m_ref, acc_ref, s_scr, *, tk, n_kv):
    tq = qt_ref.shape[1]
    zeros = jnp.zeros((HEAD_DIM, tq), BF16)
    for hh in range(N_HEADS_A):
        g = hh // GROUP_A
        parts = [zeros] * N_KV_HEADS_A
        parts[g] = qt_ref[hh * HEAD_DIM:(hh + 1) * HEAD_DIM, :]
        qp_ref[hh] = jnp.concatenate(parts, axis=0)
    m_ref[...] = jnp.full(m_ref.shape, -jnp.inf, F32)
    acc_ref[...] = jnp.zeros(acc_ref.shape, F32)

    def put_scores(slot, j, hh):
        for kt in range(tk // MXU_TILE):
            rows = pl.ds(pl.multiple_of(j * tk + kt * MXU_TILE, MXU_TILE), MXU_TILE)
            for qt in range(tq // MXU_TILE):
                cols = slice(qt * MXU_TILE, (qt + 1) * MXU_TILE)
                s_scr[slot, kt * MXU_TILE:(kt + 1) * MXU_TILE, cols] = jnp.dot(
                    k_ref[rows, :], qp_ref[hh, :, cols], preferred_element_type=F32)

    for ahead in range(GQA_AHEAD):
        put_scores(ahead, 0, ahead)

    def body(j, carry):
        start = pl.multiple_of(j * tk, tk)
        j_next = jnp.minimum(j + 1, n_kv - 1)
        for hh in range(N_HEADS_A):
            g = hh // GROUP_A
            ahead = hh + GQA_AHEAD
            nxt = (j, ahead) if ahead < N_HEADS_A else (j_next, ahead - N_HEADS_A)
            put_scores(ahead % GQA_SLOTS, *nxt)
            m_old = m_ref[hh]
            m_new = jnp.maximum(m_old, jnp.max(s_scr[hh % GQA_SLOTS], axis=0, keepdims=True))
            alpha = jnp.exp2(m_old - m_new)
            pb = jnp.exp2(s_scr[hh % GQA_SLOTS] - m_new).astype(BF16)
            m_ref[hh] = m_new
            pv_cols = []
            for qt in range(tq // MXU_TILE):
                cols = slice(qt * MXU_TILE, (qt + 1) * MXU_TILE)
                pv = None
                for kt in range(tk // MXU_TILE):
                    vc = vt_ref[g * V_ROWS:(g + 1) * V_ROWS, pl.ds(start + kt * MXU_TILE, MXU_TILE)]
                    d = jnp.dot(vc, pb[kt * MXU_TILE:(kt + 1) * MXU_TILE, cols], preferred_element_type=F32)
                    pv = d if pv is None else pv + d
                pv_cols.append(pv)
            pv = jnp.concatenate(pv_cols, axis=1)
            acc_ref[hh] = alpha * acc_ref[hh] + pv
        return carry

    lax.fori_loop(0, n_kv, body, 0, unroll=min(GQA_UNROLL, max(n_kv // 2, 1)))
    for pair in range(N_HEADS_A // 2):
        o_pair = jnp.concatenate([acc_ref[hh, :HEAD_DIM] / acc_ref[hh, HEAD_DIM:HEAD_DIM + 1]
                                  for hh in (2 * pair, 2 * pair + 1)], axis=0)
        o_ref[:, pair * LANES:(pair + 1) * LANES] = o_pair.T.astype(BF16)


def _gqa_attention(qt, k, vt):
    B, _, S = qt.shape
    tq = min(GQA_TQ, S)
    tk = min(GQA_TK, S)
    kern = functools.partial(_gqa_kernel, tk=tk, n_kv=S // tk)
    return pl.pallas_call(
        kern,
        grid=(B, S // tq),
        in_specs=[pl.BlockSpec((None, WIDTH_A, tq), lambda b, i: (b, 0, i)),
                  pl.BlockSpec((None, S, WIDTH_KV_A), lambda b, i: (b, 0, 0)),
                  pl.BlockSpec((None, N_KV_HEADS_A * V_ROWS, S), lambda b, i: (b, 0, 0))],
        out_specs=pl.BlockSpec((None, tq, WIDTH_A), lambda b, i: (b, i, 0)),
        out_shape=jax.ShapeDtypeStruct((B, S, WIDTH_A), BF16),
        scratch_shapes=[pltpu.VMEM((N_HEADS_A, WIDTH_KV_A, tq), BF16),
                        pltpu.VMEM((N_HEADS_A, 1, tq), F32),
                        pltpu.VMEM((N_HEADS_A, V_ROWS, tq), F32),
                        pltpu.VMEM((GQA_SLOTS, tk, tq), F32)],
        compiler_params=_params(2, 40),
        name="gqa",
    )(qt, k, vt)


def _na_kernel(qt_ref, kp_ref, km_ref, kn_ref, vp_ref, vm_ref, vn_ref, bias_ref, o_ref,
               kband, vband, s_scr, *, rows, rb):
    i = pl.program_id(1)
    halo = NA_MAX_ROWS * GRID_W
    main = rb * GRID_W
    if rows == rb:
        k_src, v_src, band_row0 = km_ref, vm_ref, 0
    else:
        kband[0:halo, :] = kp_ref[...]
        kband[halo:halo + main, :] = km_ref[...]
        kband[halo + main:, :] = kn_ref[...]
        vband[:, 0:halo] = vp_ref[...]
        vband[:, halo:halo + main] = vm_ref[...]
        vband[:, halo + main:] = vn_ref[...]
        k_src, v_src, band_row0 = kband, vband, NA_MAX_ROWS
    nq = NA_GROUP_ROWS * GRID_W
    nk = NA_BAND_ROWS * GRID_W
    zeros = jnp.zeros((HEAD_DIM, nq), BF16)
    n_groups = rows // NA_GROUP_ROWS

    n_local = rb // NA_GROUP_ROWS

    def geometry(gl):
        grp = i * n_local + gl
        band_start = jnp.clip(grp * NA_GROUP_ROWS - NA_MAX_ROWS // 2, 0, rows - NA_BAND_ROWS)
        base = pl.multiple_of((band_start - i * rb + band_row0) * GRID_W, nq)
        qcol = pl.multiple_of(gl * nq, nq)
        pat = jnp.where(grp == 0, 0, jnp.where(grp == n_groups - 1, 2, 1))
        return base, qcol, pat

    def put_scores(slot, geo, hh):
        base, qcol, _ = geo
        pair, sub = divmod(hh, 2)
        qh = qt_ref[hh * HEAD_DIM:(hh + 1) * HEAD_DIM, pl.ds(qcol, nq)]
        qp = jnp.concatenate([qh, zeros] if sub == 0 else [zeros, qh], axis=0)
        half = nk // MXU_COUNT
        for part in range(MXU_COUNT):
            kb = k_src[pl.ds(base + part * half, half), pair * LANES:(pair + 1) * LANES]
            s_scr[slot, part * half:(part + 1) * half, :] = jnp.dot(kb, qp, preferred_element_type=F32)

    for ahead in range(SCORE_AHEAD):
        put_scores(ahead, geometry(0), ahead)

    def group_body(gl, carry):
        geo = geometry(gl)
        geo_next = geometry(jnp.minimum(gl + 1, n_local - 1))
        base, qcol, pat = geo
        outs = []
        for hh in range(N_HEADS_B):
            ahead = hh + SCORE_AHEAD
            nxt = (geo, ahead) if ahead < N_HEADS_B else (geo_next, ahead - N_HEADS_B)
            put_scores(ahead % SCORE_SLOTS, *nxt)
            s = s_scr[hh % SCORE_SLOTS] + bias_ref[hh * 3 + pat]
            m = jnp.max(s, axis=0, keepdims=True)
            eb = jnp.exp2(s - m).astype(BF16)
            pv = None
            for c in range(nk // MXU_TILE):
                vt = v_src[hh * V_ROWS:(hh + 1) * V_ROWS, pl.ds(base + c * MXU_TILE, MXU_TILE)]
                d = jnp.dot(vt, eb[c * MXU_TILE:(c + 1) * MXU_TILE], preferred_element_type=F32)
                pv = d if pv is None else pv + d
            outs.append(pv[:HEAD_DIM] / pv[HEAD_DIM:HEAD_DIM + 1])
            if hh % 2 == 1:
                o_pair = jnp.concatenate(outs, axis=0)
                o_ref[pl.ds(qcol, nq), (hh // 2) * LANES:(hh // 2 + 1) * LANES] = o_pair.T.astype(BF16)
                outs = []
        return carry

    lax.fori_loop(0, n_local, group_body, 0, unroll=min(NA_UNROLL, max(n_local // 2, 1)))


def _na_bias_table(rpb):
    c = np.arange(GRID_W)
    kc, qc = c[:, None], c[None, :]
    col_start = np.clip(qc - NA_COLS // 2, 0, GRID_W - NA_COLS)
    in_win = (kc >= col_start) & (kc < col_start + NA_COLS)
    col_idx = np.clip(kc - qc, -(NA_COLS - 1), NA_COLS - 1) + NA_COLS - 1
    cols = jnp.where(jnp.asarray(in_win), rpb[:, :, col_idx].astype(F32) * LOG2E, -jnp.inf)
    outside = jnp.full((N_HEADS_B, GRID_W, GRID_W), -jnp.inf, F32)
    q_pos = (0, NA_MAX_ROWS // 2, NA_MAX_ROWS)
    patterns = []
    for p in range(3):
        key_rows = []
        for kr in range(NA_BAND_ROWS):
            blocks = []
            for qi in range(NA_GROUP_ROWS):
                w_start = (0, qi, NA_MAX_ROWS // 2)[p]
                inside = w_start <= kr < w_start + NA_MAX_ROWS
                blocks.append(cols[:, kr - (q_pos[p] + qi) + NA_MAX_ROWS - 1] if inside else outside)
            key_rows.append(jnp.concatenate(blocks, axis=-1))
        patterns.append(jnp.concatenate(key_rows, axis=1))
    tab = jnp.stack(patterns, axis=1)
    return tab.reshape(N_HEADS_B * 3, NA_BAND_ROWS * GRID_W, NA_GROUP_ROWS * GRID_W)


def _na_attention(qbt, kb, vbt, bias):
    B, S, W = kb.shape
    rows = S // GRID_W
    assert rows >= NA_BAND_ROWS and rows % NA_MAX_ROWS == 0
    rb = min(NA_BLOCK_ROWS, rows)
    halo = NA_MAX_ROWS * GRID_W
    per = rb // NA_MAX_ROWS
    last = rows // NA_MAX_ROWS - 1
    prev_idx = lambda i: jnp.maximum(i * per - 1, 0)
    next_idx = lambda i: jnp.minimum((i + 1) * per, last)
    k_main = pl.BlockSpec((None, rb * GRID_W, W), lambda b, i: (b, i, 0))
    k_prev = pl.BlockSpec((None, halo, W), lambda b, i: (b, prev_idx(i), 0))
    k_next = pl.BlockSpec((None, halo, W), lambda b, i: (b, next_idx(i), 0))
    q_main = pl.BlockSpec((None, W, rb * GRID_W), lambda b, i: (b, 0, i))
    vw = N_HEADS_B * V_ROWS
    v_main = pl.BlockSpec((None, vw, rb * GRID_W), lambda b, i: (b, 0, i))
    v_prev = pl.BlockSpec((None, vw, halo), lambda b, i: (b, 0, prev_idx(i)))
    v_next = pl.BlockSpec((None, vw, halo), lambda b, i: (b, 0, next_idx(i)))
    kern = functools.partial(_na_kernel, rows=rows, rb=rb)
    band_tokens = (rb + 2 * NA_MAX_ROWS) * GRID_W
    return pl.pallas_call(
        kern,
        grid=(B, rows // rb),
        in_specs=[q_main, k_prev, k_main, k_next, v_prev, v_main, v_next, _const_spec(bias.shape)],
        out_specs=k_main,
        out_shape=jax.ShapeDtypeStruct((B, S, W), BF16),
        scratch_shapes=[pltpu.VMEM((band_tokens, W), BF16), pltpu.VMEM((vw, band_tokens), BF16),
                        pltpu.VMEM((SCORE_SLOTS, NA_BAND_ROWS * GRID_W, NA_GROUP_ROWS * GRID_W), F32)],
        compiler_params=_params(2, 56),
        name="natten",
    )(qbt, kb, kb, kb, vbt, vbt, vbt, bias)


def _merge_kernel(x_ref, oa_ref, ob_ref, nw_ref, wg_ref, bg_ref, wpa_ref, wpb_ref, wout_ref, pmn_ref, o_ref):
    x = x_ref[...]
    h = _rms(x, nw_ref[...]).astype(BF16)
    g = jnp.dot(h, wg_ref[...], preferred_element_type=F32) + bg_ref[...]
    gate = jax.nn.sigmoid(g)
    pa = jnp.dot(oa_ref[...], wpa_ref[...], preferred_element_type=F32)
    pb = jnp.dot(ob_ref[...], wpb_ref[...], preferred_element_type=F32)
    mix = gate[:, :D_MODEL] * pa + gate[:, D_MODEL:] * pb
    mo = jnp.dot(mix.astype(BF16), wout_ref[...], preferred_element_type=F32)
    o_ref[...] = x + _rms(mo, pmn_ref[...])


def _merge(x2, oa2, ob2, nw, wg, bg, wpa, wpb, wout, pmn):
    n, D = x2.shape
    tm = min(PROJ_TILE, n)
    row = lambda w: pl.BlockSpec((tm, w), lambda i: (i, 0))
    return pl.pallas_call(
        _merge_kernel,
        grid=(n // tm,),
        in_specs=[row(D), row(WIDTH_A), row(WIDTH_B), _const_spec((1, D)), _const_spec(wg.shape),
                  _const_spec((1, 2 * D)), _const_spec(wpa.shape), _const_spec(wpb.shape),
                  _const_spec(wout.shape), _const_spec((1, D))],
        out_specs=row(D),
        out_shape=jax.ShapeDtypeStruct((n, D), F32),
        compiler_params=_params(1, 40),
        name="merge",
    )(x2, oa2, ob2, nw, wg, bg, wpa, wpb, wout, pmn)


def _gelu_tanh(x):
    inner = x * (GELU_C + (GELU_C * GELU_K) * (x * x))
    return x * (0.5 + 0.5 * jnp.tanh(inner))


def _ffn_kernel(xm_ref, xp_ref, xn_ref, nw_ref, wup_ref, cw_ref, cb_ref, wdn_ref, pfn_ref, o_ref,
                f_scr, *, tiles_per_seq):
    tm = xm_ref.shape[0]
    ext = tm + 2 * HALO
    j = pl.program_id(0) % tiles_per_seq
    nw = nw_ref[...]
    xm = xm_ref[...]
    hp = _rms(xp_ref[...], nw) * (j > 0).astype(F32)
    hn = _rms(xn_ref[...], nw) * (j < tiles_per_seq - 1).astype(F32)
    h_ext = jnp.concatenate([hp, _rms(xm, nw), hn], axis=0).astype(BF16)
    main = slice(HALO, HALO + tm)

    def up(c):
        cols = [slice(half + c * FFN_CHUNK, half + (c + 1) * FFN_CHUNK) for half in (0, D_FF)]
        return [(jnp.dot(h_ext, wup_ref[:, cs], preferred_element_type=F32), cs) for cs in cols]

    def conv(u, cols):
        before = pltpu.roll(u, 1, 0)[main]
        after = pltpu.roll(u, ext - 1, 0)[main]
        acc = before * cw_ref[0:1, cols] + cb_ref[:, cols]
        acc = acc + u[main] * cw_ref[1:2, cols]
        return acc + after * cw_ref[2:3, cols]

    def down(part):
        rows = slice(part[0] * FFN_CHUNK, (part[-1] + 1) * FFN_CHUNK)
        return jnp.dot(f_scr[:, rows], wdn_ref[rows, :], preferred_element_type=F32)

    n_chunks = D_FF // FFN_CHUNK
    parts = [[int(c) for c in p] for p in np.array_split(np.arange(n_chunks), FFN_DOWN_PARTS)]
    part_ending_at = {p[-1]: p for p in parts}
    y = None
    ready = None
    u_next = up(0)
    for c in range(n_chunks):
        (gate_u, gate_cols), (val_u, val_cols) = u_next
        if c + 1 < n_chunks:
            u_next = up(c + 1)
        if ready is not None:
            d = down(ready)
            y = d if y is None else y + d
        f = _gelu_tanh(conv(gate_u, gate_cols)) * conv(val_u, val_cols)
        f_scr[:, c * FFN_CHUNK:(c + 1) * FFN_CHUNK] = f.astype(BF16)
        ready = part_ending_at.get(c)
    o_ref[...] = xm + _rms(y + down(ready), pfn_ref[...])


def _ffn(x2, seq_len, nw, wup, cw, cb, wdn, pfn):
    n, D = x2.shape
    tm = min(FFN_TILE, seq_len)
    per = tm // HALO
    last = n // HALO - 1
    kern = functools.partial(_ffn_kernel, tiles_per_seq=seq_len // tm)
    return pl.pallas_call(
        kern,
        grid=(n // tm,),
        in_specs=[pl.BlockSpec((tm, D), lambda i: (i, 0)),
                  pl.BlockSpec((HALO, D), lambda i: (jnp.maximum(i * per - 1, 0), 0)),
                  pl.BlockSpec((HALO, D), lambda i: (jnp.minimum((i + 1) * per, last), 0)),
                  _const_spec((1, D)), _const_spec(wup.shape), _const_spec(cw.shape), _const_spec(cb.shape),
                  _const_spec(wdn.shape), _const_spec((1, D))],
        out_specs=pl.BlockSpec((tm, D), lambda i: (i, 0)),
        out_shape=jax.ShapeDtypeStruct((n, D), F32),
        scratch_shapes=[pltpu.VMEM((tm, D_FF), BF16)],
        compiler_params=_params(1, 48),
        name="ffn",
    )(x2, x2, x2, nw, wup, cw, cb, wdn, pfn)


def _rope_tables_t(seq_len):
    t = np.arange(seq_len)
    row = (t // GRID_W).astype(np.float32)
    col = (t % GRID_W).astype(np.float32)
    half = HEAD_DIM // 2
    freqs = (ROPE_THETA ** (-np.arange(0, half, 2, dtype=np.float32) / half)).astype(np.float32)
    ang_r = row[:, None] * freqs[None, :]
    ang_c = col[:, None] * freqs[None, :]
    ang = np.concatenate([ang_r, ang_r, ang_c, ang_c], axis=-1).astype(np.float32)
    return jnp.asarray(np.cos(ang).T, F32), jnp.asarray(np.sin(ang).T, F32)


def _prepare_weights(pre_mix_norm, w_in, b_gate, q_norm, k_norm, rpb, w_proj_a, w_proj_b, w_out,
                     post_mix_norm, pre_ffn_norm, w_up, conv_w, conv_b, w_down, post_ffn_norm):
    row = lambda v: v.reshape(1, -1).astype(F32)
    kb_lo = WIDTH_QKV_A + WIDTH_B
    kb_hi = kb_lo + WIDTH_B
    gate_lo = WIDTH_QKV_A + WIDTH_QKV_B
    return dict(
        nw1=row(pre_mix_norm),
        wt=jnp.concatenate([w_in[:, :kb_lo], w_in[:, kb_hi:gate_lo]], axis=1).T.astype(BF16),
        wkb=w_in[:, kb_lo:kb_hi].astype(BF16),
        wg=w_in[:, gate_lo:].astype(BF16),
        bg=row(b_gate),
        qn=q_norm.reshape(HEAD_DIM, 1).astype(F32),
        kn=k_norm.reshape(HEAD_DIM, 1).astype(F32),
        bias=_na_bias_table(rpb),
        wpa=w_proj_a.astype(BF16), wpb=w_proj_b.astype(BF16), wout=w_out.astype(BF16),
        pmn=row(post_mix_norm), nw2=row(pre_ffn_norm),
        wup=w_up.astype(BF16), cw=conv_w.astype(F32), cb=row(conv_b),
        wdn=w_down.astype(BF16), pfn=row(post_ffn_norm),
    )


def _encoder_layer(x, w):
    B, S, D = x.shape
    cos_t, sin_t = _rope_tables_t(S)
    qt, k, vt, qbt, kb, vbt = _project(x, w["nw1"], w["wt"], w["wkb"], cos_t, sin_t, w["qn"], w["kn"])
    oa = _gqa_attention(qt, k, vt)
    ob = _na_attention(qbt, kb, vbt, w["bias"])
    x2 = x.reshape(B * S, D)
    x2 = _merge(x2, oa.reshape(B * S, WIDTH_A), ob.reshape(B * S, WIDTH_B), w["nw1"], w["wg"], w["bg"],
                w["wpa"], w["wpb"], w["wout"], w["pmn"])
    x2 = _ffn(x2, S, w["nw2"], w["wup"], w["cw"], w["cb"], w["wdn"], w["pfn"])
    return x2.reshape(B, S, D)


def kernel(x_prompt, x_sample, pre_mix_norm, w_in, b_gate, q_norm, k_norm, rpb, w_proj_a, w_proj_b, w_out,
           post_mix_norm, pre_ffn_norm, w_up, conv_w, conv_b, w_down, post_ffn_norm):
    layers = [_prepare_weights(pre_mix_norm[l], w_in[l], b_gate[l], q_norm[l], k_norm[l], rpb[l],
                               w_proj_a[l], w_proj_b[l], w_out[l], post_mix_norm[l], pre_ffn_norm[l],
                               w_up[l], conv_w[l], conv_b[l], w_down[l], post_ffn_norm[l])
              for l in range(w_in.shape[0])]

    def run_trunk(x):
        for w in layers:
            x = _encoder_layer(x, w)
        return x

    return run_trunk(x_prompt), run_trunk(x_sample)
```

```python
import functools

import numpy as np
import jax
import jax.numpy as jnp
from jax import lax
from jax.experimental import pallas as pl
from jax.experimental.pallas import tpu as pltpu

D_MODEL = 1024
GRID_W = 64
HEAD_DIM = 64
N_HEADS_A = 8
N_KV_HEADS_A = 2
N_HEADS_B = 8
WIDTH_A = N_HEADS_A * HEAD_DIM
WIDTH_KV_A = N_KV_HEADS_A * HEAD_DIM
WIDTH_B = N_HEADS_B * HEAD_DIM
NA_MAX_ROWS = 8
NA_COLS = 16
ROPE_THETA = 10000.0
D_FF = 2816
CONV_WIDTH = 3
EPS = 1e-6
SCALE = HEAD_DIM ** -0.5
LOG2E = float(np.log2(np.e))
Q_SCALE = SCALE * LOG2E
GROUP_A = N_HEADS_A // N_KV_HEADS_A
WIDTH_QKV_A = WIDTH_A + 2 * WIDTH_KV_A
WIDTH_QKV_B = 3 * WIDTH_B

F32 = jnp.float32
BF16 = jnp.bfloat16
MIB = 1024 * 1024

SUBLANES = 8
LANES = 128
MXU_COUNT = 2
MXU_TILE = 256

ONES_ROWS = 16
V_ROWS = HEAD_DIM + ONES_ROWS

PROJ_TILE = 1024
FFN_TILE = 512
GQA_TQ = 512
GQA_TK = 256
GQA_UNROLL = 4
NA_BLOCK_ROWS = 32
NA_GROUP_ROWS = 4
NA_BAND_ROWS = 12
NA_UNROLL = 4
SCORE_SLOTS = 4
SCORE_AHEAD = 3
FFN_CHUNK = 256
HALO = SUBLANES
FFN_DOWN_PARTS = 4
GELU_C = float(np.sqrt(2.0 / np.pi))
GELU_K = 0.044715


def _const_spec(shape):
    n = len(shape)
    return pl.BlockSpec(shape, lambda *_: (0,) * n, pipeline_mode=pl.Buffered(1))


def _params(n_axes, vmem_mib):
    return pltpu.CompilerParams(dimension_semantics=("parallel",) * n_axes,
                                vmem_limit_bytes=vmem_mib * MIB)


def _rms(x, w):
    return x * lax.rsqrt(jnp.mean(x * x, axis=-1, keepdims=True) + EPS) * w


def _proj_kernel(x_ref, nw_ref, wt_ref, wkb_ref, cos_ref, sin_ref, qn_ref, kn_ref,
                 qt_ref, k_ref, vt_ref, qbt_ref, kb_ref, vbt_ref):
    h = _rms(x_ref[...], nw_ref[...]).astype(BF16)
    pt = lax.dot_general(wt_ref[...], h, (((1,), (1,)), ((), ())), preferred_element_type=F32)
    cos = cos_ref[...]
    sin = sin_ref[...]

    def norm_rope(t, w):
        t = t * lax.rsqrt(jnp.mean(t * t, axis=0, keepdims=True) + EPS) * w
        q4 = HEAD_DIM // 4
        rot = jnp.concatenate([-t[q4:2 * q4], t[0:q4], -t[3 * q4:4 * q4], t[2 * q4:3 * q4]], axis=0)
        return t * cos + rot * sin

    qn = qn_ref[...]
    kn = kn_ref[...]
    for hh in range(N_HEADS_A):
        lo = hh * HEAD_DIM
        qt_ref[lo:lo + HEAD_DIM, :] = (norm_rope(pt[lo:lo + HEAD_DIM], qn) * Q_SCALE).astype(BF16)
    kt = jnp.concatenate(
        [norm_rope(pt[WIDTH_A + g * HEAD_DIM:WIDTH_A + (g + 1) * HEAD_DIM], kn) for g in range(N_KV_HEADS_A)],
        axis=0)
    k_ref[...] = kt.T.astype(BF16)
    ones = jnp.ones((ONES_ROWS, pt.shape[1]), BF16)

    def put_values(ref, first_row, n_heads):
        for hh in range(n_heads):
            lo = first_row + hh * HEAD_DIM
            ref[hh * V_ROWS:hh * V_ROWS + HEAD_DIM, :] = pt[lo:lo + HEAD_DIM].astype(BF16)
            ref[hh * V_ROWS + HEAD_DIM:(hh + 1) * V_ROWS, :] = ones

    put_values(vt_ref, WIDTH_A + WIDTH_KV_A, N_KV_HEADS_A)
    qbt_ref[...] = (pt[WIDTH_QKV_A:WIDTH_QKV_A + WIDTH_B] * Q_SCALE).astype(BF16)
    put_values(vbt_ref, WIDTH_QKV_A + WIDTH_B, N_HEADS_B)
    kb_ref[...] = jnp.dot(h, wkb_ref[...], preferred_element_type=F32).astype(BF16)


def _project(x, nw, wt, wkb, cos_t, sin_t, qn, kn):
    B, S, D = x.shape
    tm = min(PROJ_TILE, S)
    grid = (B, S // tm)
    row_blk = lambda w: pl.BlockSpec((None, tm, w), lambda b, i: (b, i, 0))
    col_blk = lambda w: pl.BlockSpec((None, w, tm), lambda b, i: (b, 0, i))
    return pl.pallas_call(
        _proj_kernel,
        grid=grid,
        in_specs=[row_blk(D), _const_spec((1, D)), _const_spec(wt.shape), _const_spec(wkb.shape),
                  pl.BlockSpec((HEAD_DIM, tm), lambda b, i: (0, i)),
                  pl.BlockSpec((HEAD_DIM, tm), lambda b, i: (0, i)),
                  _const_spec((HEAD_DIM, 1)), _const_spec((HEAD_DIM, 1))],
        out_specs=[col_blk(WIDTH_A), row_blk(WIDTH_KV_A), col_blk(N_KV_HEADS_A * V_ROWS),
                   col_blk(WIDTH_B), row_blk(WIDTH_B), col_blk(N_HEADS_B * V_ROWS)],
        out_shape=[jax.ShapeDtypeStruct((B, WIDTH_A, S), BF16),
                   jax.ShapeDtypeStruct((B, S, WIDTH_KV_A), BF16),
                   jax.ShapeDtypeStruct((B, N_KV_HEADS_A * V_ROWS, S), BF16),
                   jax.ShapeDtypeStruct((B, WIDTH_B, S), BF16),
                   jax.ShapeDtypeStruct((B, S, WIDTH_B), BF16),
                   jax.ShapeDtypeStruct((B, N_HEADS_B * V_ROWS, S), BF16)],
        compiler_params=_params(2, 40),
        name="proj",
    )(x, nw, wt, wkb, cos_t, sin_t, qn, kn)


def _gqa_kernel(qt_ref, k_ref, vt_ref, o_ref, qp_ref, m_ref, acc_ref, s_scr, *, tk, n_kv):
    tq = qt_ref.shape[1]
    zeros = jnp.zeros((HEAD_DIM, tq), BF16)
    for hh in range(N_HEADS_A):
        g = hh // GROUP_A
        parts = [zeros] * N_KV_HEADS_A
        parts[g] = qt_ref[hh * HEAD_DIM:(hh + 1) * HEAD_DIM, :]
        qp_ref[hh] = jnp.concatenate(parts, axis=0)
    m_ref[...] = jnp.full(m_ref.shape, -jnp.inf, F32)
    acc_ref[...] = jnp.zeros(acc_ref.shape, F32)

    def put_scores(slot, j, hh):
        for kt in range(tk // MXU_TILE):
            rows = pl.ds(pl.multiple_of(j * tk + kt * MXU_TILE, MXU_TILE), MXU_TILE)
            for qt in range(tq // MXU_TILE):
                cols = slice(qt * MXU_TILE, (qt + 1) * MXU_TILE)
                s_scr[slot, kt * MXU_TILE:(kt + 1) * MXU_TILE, cols] = jnp.dot(
                    k_ref[rows, :], qp_ref[hh, :, cols], preferred_element_type=F32)

    for ahead in range(SCORE_AHEAD):
        put_scores(ahead, 0, ahead)

    def body(j, carry):
        start = pl.multiple_of(j * tk, tk)
        j_next = jnp.minimum(j + 1, n_kv - 1)
        for hh in range(N_HEADS_A):
            g = hh // GROUP_A
            ahead = hh + SCORE_AHEAD
            nxt = (j, ahead) if ahead < N_HEADS_A else (j_next, ahead - N_HEADS_A)
            put_scores(ahead % SCORE_SLOTS, *nxt)
            m_old = m_ref[hh]
            m_new = jnp.maximum(m_old, jnp.max(s_scr[hh % SCORE_SLOTS], axis=0, keepdims=True))
            alpha = jnp.exp2(m_old - m_new)
            pb = jnp.exp2(s_scr[hh % SCORE_SLOTS] - m_new).astype(BF16)
            m_ref[hh] = m_new
            pv_cols = []
            for qt in range(tq // MXU_TILE):
                cols = slice(qt * MXU_TILE, (qt + 1) * MXU_TILE)
                pv = None
                for kt in range(tk // MXU_TILE):
                    vc = vt_ref[g * V_ROWS:(g + 1) * V_ROWS, pl.ds(start + kt * MXU_TILE, MXU_TILE)]
                    d = jnp.dot(vc, pb[kt * MXU_TILE:(kt + 1) * MXU_TILE, cols], preferred_element_type=F32)
                    pv = d if pv is None else pv + d
                pv_cols.append(pv)
            pv = jnp.concatenate(pv_cols, axis=1)
            acc_ref[hh] = alpha * acc_ref[hh] + pv
        return carry

    lax.fori_loop(0, n_kv, body, 0, unroll=min(GQA_UNROLL, max(n_kv // 2, 1)))
    for pair in range(N_HEADS_A // 2):
        o_pair = jnp.concatenate([acc_ref[hh, :HEAD_DIM] / acc_ref[hh, HEAD_DIM:HEAD_DIM + 1]
                                  for hh in (2 * pair, 2 * pair + 1)], axis=0)
        o_ref[pair * LANES:(pair + 1) * LANES, :] = o_pair.astype(BF16)


def _gqa_attention(qt, k, vt):
    B, _, S = qt.shape
    tq = min(GQA_TQ, S)
    tk = min(GQA_TK, S)
    kern = functools.partial(_gqa_kernel, tk=tk, n_kv=S // tk)
    return pl.pallas_call(
        kern,
        grid=(B, S // tq),
        in_specs=[pl.BlockSpec((None, WIDTH_A, tq), lambda b, i: (b, 0, i)),
                  pl.BlockSpec((None, S, WIDTH_KV_A), lambda b, i: (b, 0, 0)),
                  pl.BlockSpec((None, N_KV_HEADS_A * V_ROWS, S), lambda b, i: (b, 0, 0))],
        out_specs=pl.BlockSpec((None, WIDTH_A, tq), lambda b, i: (b, 0, i)),
        out_shape=jax.ShapeDtypeStruct((B, WIDTH_A, S), BF16),
        scratch_shapes=[pltpu.VMEM((N_HEADS_A, WIDTH_KV_A, tq), BF16),
                        pltpu.VMEM((N_HEADS_A, 1, tq), F32),
                        pltpu.VMEM((N_HEADS_A, V_ROWS, tq), F32),
                        pltpu.VMEM((SCORE_SLOTS, tk, tq), F32)],
        compiler_params=_params(2, 40),
        name="gqa",
    )(qt, k, vt)


def _na_kernel(qt_ref, kp_ref, km_ref, kn_ref, vp_ref, vm_ref, vn_ref, bias_ref, o_ref,
               kband, vband, s_scr, *, rows, rb):
    i = pl.program_id(1)
    halo = NA_MAX_ROWS * GRID_W
    main = rb * GRID_W
    if rows == rb:
        k_src, v_src, band_row0 = km_ref, vm_ref, 0
    else:
        kband[0:halo, :] = kp_ref[...]
        kband[halo:halo + main, :] = km_ref[...]
        kband[halo + main:, :] = kn_ref[...]
        vband[:, 0:halo] = vp_ref[...]
        vband[:, halo:halo + main] = vm_ref[...]
        vband[:, halo + main:] = vn_ref[...]
        k_src, v_src, band_row0 = kband, vband, NA_MAX_ROWS
    nq = NA_GROUP_ROWS * GRID_W
    nk = NA_BAND_ROWS * GRID_W
    zeros = jnp.zeros((HEAD_DIM, nq), BF16)
    n_groups = rows // NA_GROUP_ROWS

    n_local = rb // NA_GROUP_ROWS

    def geometry(gl):
        grp = i * n_local + gl
        band_start = jnp.clip(grp * NA_GROUP_ROWS - NA_MAX_ROWS // 2, 0, rows - NA_BAND_ROWS)
        base = pl.multiple_of((band_start - i * rb + band_row0) * GRID_W, nq)
        qcol = pl.multiple_of(gl * nq, nq)
        pat = jnp.where(grp == 0, 0, jnp.where(grp == n_groups - 1, 2, 1))
        return base, qcol, pat

    def put_scores(slot, geo, hh):
        base, qcol, _ = geo
        pair, sub = divmod(hh, 2)
        qh = qt_ref[hh * HEAD_DIM:(hh + 1) * HEAD_DIM, pl.ds(qcol, nq)]
        qp = jnp.concatenate([qh, zeros] if sub == 0 else [zeros, qh], axis=0)
        half = nk // MXU_COUNT
        for part in range(MXU_COUNT):
            kb = k_src[pl.ds(base + part * half, half), pair * LANES:(pair + 1) * LANES]
            s_scr[slot, part * half:(part + 1) * half, :] = jnp.dot(kb, qp, preferred_element_type=F32)

    for ahead in range(SCORE_AHEAD):
        put_scores(ahead, geometry(0), ahead)

    def group_body(gl, carry):
        geo = geometry(gl)
        geo_next = geometry(jnp.minimum(gl + 1, n_local - 1))
        base, qcol, pat = geo
        outs = []
        for hh in range(N_HEADS_B):
            ahead = hh + SCORE_AHEAD
            nxt = (geo, ahead) if ahead < N_HEADS_B else (geo_next, ahead - N_HEADS_B)
            put_scores(ahead % SCORE_SLOTS, *nxt)
            s = s_scr[hh % SCORE_SLOTS] + bias_ref[hh * 3 + pat]
            m = jnp.max(s, axis=0, keepdims=True)
            eb = jnp.exp2(s - m).astype(BF16)
            pv = None
            for c in range(nk // MXU_TILE):
                vt = v_src[hh * V_ROWS:(hh + 1) * V_ROWS, pl.ds(base + c * MXU_TILE, MXU_TILE)]
                d = jnp.dot(vt, eb[c * MXU_TILE:(c + 1) * MXU_TILE], preferred_element_type=F32)
                pv = d if pv is None else pv + d
            outs.append(pv[:HEAD_DIM] / pv[HEAD_DIM:HEAD_DIM + 1])
            if hh % 2 == 1:
                o_pair = jnp.concatenate(outs, axis=0)
                o_ref[pl.ds(qcol, nq), (hh // 2) * LANES:(hh // 2 + 1) * LANES] = o_pair.T.astype(BF16)
                outs = []
        return carry

    lax.fori_loop(0, n_local, group_body, 0, unroll=min(NA_UNROLL, max(n_local // 2, 1)))


def _na_bias_table(rpb):
    c = np.arange(GRID_W)
    kc, qc = c[:, None], c[None, :]
    col_start = np.clip(qc - NA_COLS // 2, 0, GRID_W - NA_COLS)
    in_win = (kc >= col_start) & (kc < col_start + NA_COLS)
    col_idx = np.clip(kc - qc, -(NA_COLS - 1), NA_COLS - 1) + NA_COLS - 1
    cols = jnp.where(jnp.asarray(in_win), rpb[:, :, col_idx].astype(F32) * LOG2E, -jnp.inf)
    outside = jnp.full((N_HEADS_B, GRID_W, GRID_W), -jnp.inf, F32)
    q_pos = (0, NA_MAX_ROWS // 2, NA_MAX_ROWS)
    patterns = []
    for p in range(3):
        key_rows = []
        for kr in range(NA_BAND_ROWS):
            blocks = []
            for qi in range(NA_GROUP_ROWS):
                w_start = (0, qi, NA_MAX_ROWS // 2)[p]
                inside = w_start <= kr < w_start + NA_MAX_ROWS
                blocks.append(cols[:, kr - (q_pos[p] + qi) + NA_MAX_ROWS - 1] if inside else outside)
            key_rows.append(jnp.concatenate(blocks, axis=-1))
        patterns.append(jnp.concatenate(key_rows, axis=1))
    tab = jnp.stack(patterns, axis=1)
    return tab.reshape(N_HEADS_B * 3, NA_BAND_ROWS * GRID_W, NA_GROUP_ROWS * GRID_W)


def _na_attention(qbt, kb, vbt, bias):
    B, S, W = kb.shape
    rows = S // GRID_W
    assert rows >= NA_BAND_ROWS and rows % NA_MAX_ROWS == 0
    rb = min(NA_BLOCK_ROWS, rows)
    halo = NA_MAX_ROWS * GRID_W
    per = rb // NA_MAX_ROWS
    last = rows // NA_MAX_ROWS - 1
    prev_idx = lambda i: jnp.maximum(i * per - 1, 0)
    next_idx = lambda i: jnp.minimum((i + 1) * per, last)
    k_main = pl.BlockSpec((None, rb * GRID_W, W), lambda b, i: (b, i, 0))
    k_prev = pl.BlockSpec((None, halo, W), lambda b, i: (b, prev_idx(i), 0))
    k_next = pl.BlockSpec((None, halo, W), lambda b, i: (b, next_idx(i), 0))
    q_main = pl.BlockSpec((None, W, rb * GRID_W), lambda b, i: (b, 0, i))
    vw = N_HEADS_B * V_ROWS
    v_main = pl.BlockSpec((None, vw, rb * GRID_W), lambda b, i: (b, 0, i))
    v_prev = pl.BlockSpec((None, vw, halo), lambda b, i: (b, 0, prev_idx(i)))
    v_next = pl.BlockSpec((None, vw, halo), lambda b, i: (b, 0, next_idx(i)))
    kern = functools.partial(_na_kernel, rows=rows, rb=rb)
    band_tokens = (rb + 2 * NA_MAX_ROWS) * GRID_W
    return pl.pallas_call(
        kern,
        grid=(B, rows // rb),
        in_specs=[q_main, k_prev, k_main, k_next, v_prev, v_main, v_next, _const_spec(bias.shape)],
        out_specs=k_main,
        out_shape=jax.ShapeDtypeStruct((B, S, W), BF16),
        scratch_shapes=[pltpu.VMEM((band_tokens, W), BF16), pltpu.VMEM((vw, band_tokens), BF16),
                        pltpu.VMEM((SCORE_SLOTS, NA_BAND_ROWS * GRID_W, NA_GROUP_ROWS * GRID_W), F32)],
        compiler_params=_params(2, 56),
        name="natten",
    )(qbt, kb, kb, kb, vbt, vbt, vbt, bias)


def _merge_kernel(x_ref, oa_ref, ob_ref, nw_ref, wg_ref, bg_ref, wpa_ref, wpb_ref, wout_ref, pmn_ref, o_ref):
    x = x_ref[...]
    h = _rms(x, nw_ref[...]).astype(BF16)
    g = jnp.dot(h, wg_ref[...], preferred_element_type=F32) + bg_ref[...]
    gate = jax.nn.sigmoid(g)
    pa = lax.dot_general(oa_ref[...], wpa_ref[...], (((0,), (0,)), ((), ())), preferred_element_type=F32)
    pb = jnp.dot(ob_ref[...], wpb_ref[...], preferred_element_type=F32)
    mix = gate[:, :D_MODEL] * pa + gate[:, D_MODEL:] * pb
    mo = jnp.dot(mix.astype(BF16), wout_ref[...], preferred_element_type=F32)
    o_ref[...] = x + _rms(mo, pmn_ref[...])


def _merge(x2, oat, ob2, nw, wg, bg, wpa, wpb, wout, pmn):
    n, D = x2.shape
    seq_len = oat.shape[2]
    tm = min(PROJ_TILE, seq_len)
    per = seq_len // tm
    row = lambda w: pl.BlockSpec((tm, w), lambda i: (i, 0))
    oat_spec = pl.BlockSpec((None, WIDTH_A, tm), lambda i: (i // per, 0, i % per))
    return pl.pallas_call(
        _merge_kernel,
        grid=(n // tm,),
        in_specs=[row(D), oat_spec, row(WIDTH_B), _const_spec((1, D)), _const_spec(wg.shape),
                  _const_spec((1, 2 * D)), _const_spec(wpa.shape), _const_spec(wpb.shape),
                  _const_spec(wout.shape), _const_spec((1, D))],
        out_specs=row(D),
        out_shape=jax.ShapeDtypeStruct((n, D), F32),
        compiler_params=_params(1, 40),
        name="merge",
    )(x2, oat, ob2, nw, wg, bg, wpa, wpb, wout, pmn)


def _gelu_tanh(x):
    inner = x * (GELU_C + (GELU_C * GELU_K) * (x * x))
    return x * (0.5 + 0.5 * jnp.tanh(inner))


def _ffn_kernel(xm_ref, xp_ref, xn_ref, nw_ref, wup_ref, cw_ref, cb_ref, wdn_ref, pfn_ref, o_ref,
                f_scr, *, tiles_per_seq):
    tm = xm_ref.shape[0]
    ext = tm + 2 * HALO
    j = pl.program_id(0) % tiles_per_seq
    nw = nw_ref[...]
    xm = xm_ref[...]
    hp = _rms(xp_ref[...], nw) * (j > 0).astype(F32)
    hn = _rms(xn_ref[...], nw) * (j < tiles_per_seq - 1).astype(F32)
    h_ext = jnp.concatenate([hp, _rms(xm, nw), hn], axis=0).astype(BF16)
    main = slice(HALO, HALO + tm)

    def up(c):
        cols = [slice(half + c * FFN_CHUNK, half + (c + 1) * FFN_CHUNK) for half in (0, D_FF)]
        return [(jnp.dot(h_ext, wup_ref[:, cs], preferred_element_type=F32), cs) for cs in cols]

    def conv(u, cols):
        before = pltpu.roll(u, 1, 0)[main]
        after = pltpu.roll(u, ext - 1, 0)[main]
        acc = before * cw_ref[0:1, cols] + cb_ref[:, cols]
        acc = acc + u[main] * cw_ref[1:2, cols]
        return acc + after * cw_ref[2:3, cols]

    def down(part):
        rows = slice(part[0] * FFN_CHUNK, (part[-1] + 1) * FFN_CHUNK)
        return jnp.dot(f_scr[:, rows], wdn_ref[rows, :], preferred_element_type=F32)

    n_chunks = D_FF // FFN_CHUNK
    parts = [[int(c) for c in p] for p in np.array_split(np.arange(n_chunks), FFN_DOWN_PARTS)]
    part_ending_at = {p[-1]: p for p in parts}
    y = None
    ready = None
    u_next = up(0)
    for c in range(n_chunks):
        (gate_u, gate_cols), (val_u, val_cols) = u_next
        if c + 1 < n_chunks:
            u_next = up(c + 1)
        if ready is not None:
            d = down(ready)
            y = d if y is None else y + d
        f = _gelu_tanh(conv(gate_u, gate_cols)) * conv(val_u, val_cols)
        f_scr[:, c * FFN_CHUNK:(c + 1) * FFN_CHUNK] = f.astype(BF16)
        ready = part_ending_at.get(c)
    o_ref[...] = xm + _rms(y + down(ready), pfn_ref[...])


def _ffn(x2, seq_len, nw, wup, cw, cb, wdn, pfn):
    n, D = x2.shape
    tm = min(FFN_TILE, seq_len)
    per = tm // HALO
    last = n // HALO - 1
    kern = functools.partial(_ffn_kernel, tiles_per_seq=seq_len // tm)
    return pl.pallas_call(
        kern,
        grid=(n // tm,),
        in_specs=[pl.BlockSpec((tm, D), lambda i: (i, 0)),
                  pl.BlockSpec((HALO, D), lambda i: (jnp.maximum(i * per - 1, 0), 0)),
                  pl.BlockSpec((HALO, D), lambda i: (jnp.minimum((i + 1) * per, last), 0)),
                  _const_spec((1, D)), _const_spec(wup.shape), _const_spec(cw.shape), _const_spec(cb.shape),
                  _const_spec(wdn.shape), _const_spec((1, D))],
        out_specs=pl.BlockSpec((tm, D), lambda i: (i, 0)),
        out_shape=jax.ShapeDtypeStruct((n, D), F32),
        scratch_shapes=[pltpu.VMEM((tm, D_FF), BF16)],
        compiler_params=_params(1, 48),
        name="ffn",
    )(x2, x2, x2, nw, wup, cw, cb, wdn, pfn)


def _rope_tables_t(seq_len):
    t = np.arange(seq_len)
    row = (t // GRID_W).astype(np.float32)
    col = (t % GRID_W).astype(np.float32)
    half = HEAD_DIM // 2
    freqs = (ROPE_THETA ** (-np.arange(0, half, 2, dtype=np.float32) / half)).astype(np.float32)
    ang_r = row[:, None] * freqs[None, :]
    ang_c = col[:, None] * freqs[None, :]
    ang = np.concatenate([ang_r, ang_r, ang_c, ang_c], axis=-1).astype(np.float32)
    return jnp.asarray(np.cos(ang).T, F32), jnp.asarray(np.sin(ang).T, F32)


def _prepare_weights(pre_mix_norm, w_in, b_gate, q_norm, k_norm, rpb, w_proj_a, w_proj_b, w_out,
                     post_mix_norm, pre_ffn_norm, w_up, conv_w, conv_b, w_down, post_ffn_norm):
    row = lambda v: v.reshape(1, -1).astype(F32)
    kb_lo = WIDTH_QKV_A + WIDTH_B
    kb_hi = kb_lo + WIDTH_B
    gate_lo = WIDTH_QKV_A + WIDTH_QKV_B
    return dict(
        nw1=row(pre_mix_norm),
        wt=jnp.concatenate([w_in[:, :kb_lo], w_in[:, kb_hi:gate_lo]], axis=1).T.astype(BF16),
        wkb=w_in[:, kb_lo:kb_hi].astype(BF16),
        wg=w_in[:, gate_lo:].astype(BF16),
        bg=row(b_gate),
        qn=q_norm.reshape(HEAD_DIM, 1).astype(F32),
        kn=k_norm.reshape(HEAD_DIM, 1).astype(F32),
        bias=_na_bias_table(rpb),
        wpa=w_proj_a.astype(BF16), wpb=w_proj_b.astype(BF16), wout=w_out.astype(BF16),
        pmn=row(post_mix_norm), nw2=row(pre_ffn_norm),
        wup=w_up.astype(BF16), cw=conv_w.astype(F32), cb=row(conv_b),
        wdn=w_down.astype(BF16), pfn=row(post_ffn_norm),
    )


def _encoder_layer(x, w):
    B, S, D = x.shape
    cos_t, sin_t = _rope_tables_t(S)
    qt, k, vt, qbt, kb, vbt = _project(x, w["nw1"], w["wt"], w["wkb"], cos_t, sin_t, w["qn"], w["kn"])
    oa = _gqa_attention(qt, k, vt)
    ob = _na_attention(qbt, kb, vbt, w["bias"])
    x2 = x.reshape(B * S, D)
    x2 = _merge(x2, oa, ob.reshape(B * S, WIDTH_B), w["nw1"], w["wg"], w["bg"],
                w["wpa"], w["wpb"], w["wout"], w["pmn"])
    x2 = _ffn(x2, S, w["nw2"], w["wup"], w["cw"], w["cb"], w["wdn"], w["pfn"])
    return x2.reshape(B, S, D)


def kernel(x_prompt, x_sample, pre_mix_norm, w_in, b_gate, q_norm, k_norm, rpb, w_proj_a, w_proj_b, w_out,
           post_mix_norm, pre_ffn_norm, w_up, conv_w, conv_b, w_down, post_ffn_norm):
    layers = [_prepare_weights(pre_mix_norm[l], w_in[l], b_gate[l], q_norm[l], k_norm[l], rpb[l],
                               w_proj_a[l], w_proj_b[l], w_out[l], post_mix_norm[l], pre_ffn_norm[l],
                               w_up[l], conv_w[l], conv_b[l], w_down[l], post_ffn_norm[l])
              for l in range(w_in.shape[0])]

    def run_trunk(x):
        for w in layers:
            x = _encoder_layer(x, w)
        return x

    return run_trunk(x_prompt), run_trunk(x_sample)
```
